```python
import jax, jax.numpy as jnp
from jax import lax
import numpy as np

D_MODEL = 1024
BATCH = 16
SEQ = 4096
DEPTH = 1

N_META = 16
CHUNK = 64
N_PAD = CHUNK - N_META
DN_HEADS = 4
DN_DK = 128
DN_DV = 128
CONV_K = 4
GLA_HEADS = 4
GLA_DK = 64
GLA_DV = 128
GLA_RANK = 16
GLA_NORMALIZER = 16.0
D_FF = 4 * D_MODEL
EPS = 1e-6

DN_QK = DN_HEADS * DN_DK
DN_V = DN_HEADS * DN_DV
GLA_QK = GLA_HEADS * GLA_DK
GLA_V = GLA_HEADS * GLA_DV
MIX_WIDTH = DN_V + GLA_V
SPLITS = (DN_QK, DN_QK, DN_V, DN_V, DN_HEADS, DN_HEADS, GLA_QK, GLA_QK, GLA_V, GLA_V, GLA_RANK)
IN_WIDTH = DN_QK * 2 + DN_V * 2 + DN_HEADS * 2 + GLA_QK * 2 + GLA_V * 2 + GLA_RANK

kernel_name = "hymba_gdn_gla_hybrid"


def rmsnorm(x, g):
    xf = x.astype(jnp.float32)
    y = xf * lax.rsqrt(jnp.mean(xf * xf, axis=-1, keepdims=True) + EPS)
    return (y * g.astype(jnp.float32)).astype(x.dtype)


def l2norm(x):
    xf = x.astype(jnp.float32)
    return xf * lax.rsqrt(jnp.sum(xf * xf, axis=-1, keepdims=True) + EPS)


def causal_conv(x, w):
    K = w.shape[0]
    T = x.shape[1]
    xp = jnp.pad(x, ((0, 0), (K - 1, 0), (0, 0)))
    y = xp[:, 0:T] * w[0]
    for i in range(1, K):
        y = y + xp[:, i:i + T] * w[i]
    return y


def to_head_chunks(x, n_heads):
    B, T, W = x.shape
    return x.reshape(B, T // CHUNK, CHUNK, n_heads, W // n_heads).transpose(0, 3, 1, 2, 4)


def scalar_chunks(x):
    B, T, H = x.shape
    return x.reshape(B, T // CHUNK, CHUNK, H).transpose(0, 3, 1, 2)


def from_head_chunks(o):
    B, H, N, C, d = o.shape
    return o.transpose(0, 2, 3, 1, 4).reshape(B, N * C, H, d)


def gated_delta_chunked(q, k, v, beta, g):
    B, H, N, C, dk = q.shape
    dv = v.shape[-1]
    q = q * (dk ** -0.5)
    gc = jnp.cumsum(g, axis=-1)
    tril = jnp.tril(jnp.ones((C, C), dtype=bool))
    strict = jnp.tril(jnp.ones((C, C), dtype=bool), -1)
    decay = jnp.exp(jnp.where(tril, gc[..., :, None] - gc[..., None, :], -jnp.inf))
    kb = k * beta[..., None]
    a = jnp.einsum('bhncd,bhnsd->bhncs', kb, k) * decay
    m = jnp.eye(C, dtype=jnp.float32) + jnp.where(strict, a, 0.0)
    rhs = jnp.concatenate([v * beta[..., None], kb * jnp.exp(gc)[..., None]], axis=-1)
    sol = lax.linalg.triangular_solve(m, rhs, left_side=True, lower=True, unit_diagonal=True)
    u = sol[..., :dv]
    w = sol[..., dv:]
    attn = jnp.einsum('bhncd,bhnsd->bhncs', q, k) * decay
    qg = q * jnp.exp(gc)[..., None]
    kd = k * jnp.exp(gc[..., -1:] - gc)[..., None]
    glast = jnp.exp(gc[..., -1])

    def step(S, xs):
        u_c, w_c, attn_c, qg_c, kd_c, gl_c = xs
        v_new = u_c - jnp.einsum('bhcd,bhde->bhce', w_c, S)
        o = jnp.einsum('bhcd,bhde->bhce', qg_c, S) + jnp.einsum('bhcs,bhse->bhce', attn_c, v_new)
        S = S * gl_c[..., None, None] + jnp.einsum('bhcd,bhce->bhde', kd_c, v_new)
        return S, o

    xs = (jnp.moveaxis(u, 2, 0), jnp.moveaxis(w, 2, 0), jnp.moveaxis(attn, 2, 0),
          jnp.moveaxis(qg, 2, 0), jnp.moveaxis(kd, 2, 0), jnp.moveaxis(glast, 2, 0))
    S0 = jnp.zeros((B, H, dk, dv), jnp.float32)
    _, o = lax.scan(step, S0, xs)
    return jnp.moveaxis(o, 0, 2)


def gla_chunked(q, k, v, g):
    B, H, N, C, dk = q.shape
    dv = v.shape[-1]
    q = q * (dk ** -0.5)
    b = jnp.cumsum(g, axis=-2)
    bref = b[..., C // 2:C // 2 + 1, :]
    qi = q * jnp.exp(b - bref)
    ki = k * jnp.exp(bref - b)
    tril = jnp.tril(jnp.ones((C, C), dtype=bool))
    A = jnp.where(tril, jnp.einsum('bhncd,bhnsd->bhncs', qi, ki), 0.0)
    o_intra = jnp.einsum('bhncs,bhnse->bhnce', A, v)
    qg = q * jnp.exp(b)
    kd = k * jnp.exp(b[..., -1:, :] - b)
    glast = jnp.exp(b[..., -1, :])

    def step(S, xs):
        qg_c, kd_c, v_c, gl_c = xs
        o = jnp.einsum('bhcd,bhde->bhce', qg_c, S)
        S = S * gl_c[..., None] + jnp.einsum('bhcd,bhce->bhde', kd_c, v_c)
        return S, o

    xs = (jnp.moveaxis(qg, 2, 0), jnp.moveaxis(kd, 2, 0), jnp.moveaxis(v, 2, 0), jnp.moveaxis(glast, 2, 0))
    S0 = jnp.zeros((B, H, dk, dv), jnp.float32)
    _, o_inter = lax.scan(step, S0, xs)
    return o_intra + jnp.moveaxis(o_inter, 0, 2)


def hybrid_layer(x, valid, norm1_g, w_in, conv_w, a_log, dt_bias, dn_norm_g,
                 gla_w2, gla_b, gla_norm_g, w_out, norm2_g, w_up, w_down):
    B, T, _ = x.shape
    vmask = valid[None, :, None]
    h = jnp.where(vmask, rmsnorm(x, norm1_g), 0).astype(x.dtype)
    proj = h @ w_in
    offs = [0]
    for s in SPLITS:
        offs.append(offs[-1] + s)
    (dq, dk_, dv, dz, db, da, gq, gk, gv, gr, glr) = [proj[..., offs[i]:offs[i + 1]] for i in range(len(SPLITS))]

    qkv = jax.nn.silu(causal_conv(jnp.concatenate([dq, dk_, dv], axis=-1), conv_w))
    q_dn = l2norm(to_head_chunks(qkv[..., :DN_QK], DN_HEADS))
    k_dn = l2norm(to_head_chunks(qkv[..., DN_QK:2 * DN_QK], DN_HEADS))
    v_dn = to_head_chunks(qkv[..., 2 * DN_QK:], DN_HEADS).astype(jnp.float32)
    beta = jax.nn.sigmoid(db.astype(jnp.float32))
    g_dn = -jnp.exp(a_log.astype(jnp.float32)) * jax.nn.softplus(da.astype(jnp.float32) + dt_bias.astype(jnp.float32))
    g_dn = jnp.where(vmask, g_dn, 0.0)
    o_dn = from_head_chunks(gated_delta_chunked(q_dn, k_dn, v_dn, scalar_chunks(beta), scalar_chunks(g_dn)))
    o_dn = rmsnorm(o_dn, dn_norm_g) * jax.nn.silu(dz.reshape(B, T, DN_HEADS, DN_DV).astype(jnp.float32))
    o_dn = o_dn.reshape(B, T, DN_V)

    g_gla = jax.nn.log_sigmoid((glr @ gla_w2 + gla_b).astype(jnp.float32)) / GLA_NORMALIZER
    g_gla = jnp.where(vmask, g_gla, 0.0)
    o_gla = gla_chunked(to_head_chunks(gq, GLA_HEADS).astype(jnp.float32),
                        to_head_chunks(gk, GLA_HEADS).astype(jnp.float32),
                        to_head_chunks(gv, GLA_HEADS).astype(jnp.float32),
                        to_head_chunks(g_gla, GLA_HEADS))
    o_gla = rmsnorm(from_head_chunks(o_gla), gla_norm_g) * jax.nn.silu(gr.reshape(B, T, GLA_HEADS, GLA_DV).astype(jnp.float32))
    o_gla = o_gla.reshape(B, T, GLA_V)

    mix = jnp.concatenate([o_dn, o_gla], axis=-1).astype(x.dtype)
    x = x + mix @ w_out

    h2 = rmsnorm(x, norm2_g)
    x = x + jnp.square(jax.nn.relu(h2 @ w_up)) @ w_down
    return x


def setup_inputs(seed: int = 0) -> dict:
    key = jax.random.key(seed)
    ks = jax.random.split(key, 20)
    f32 = jnp.float32
    x = jax.random.normal(ks[0], (BATCH, SEQ, D_MODEL), f32)
    meta_tokens = jax.random.normal(ks[1], (N_META, D_MODEL), f32)
    norm1_g = 1.0 + 0.02 * jax.random.normal(ks[2], (DEPTH, D_MODEL), f32)
    w_in = jax.random.normal(ks[3], (DEPTH, D_MODEL, IN_WIDTH), f32) * D_MODEL ** -0.5
    conv_w = jax.random.normal(ks[4], (DEPTH, CONV_K, 2 * DN_QK + DN_V), f32) * CONV_K ** -0.5
    a_log = jnp.log(jax.random.uniform(ks[5], (DEPTH, DN_HEADS), f32, 1.0, 16.0))
    dt = jnp.exp(jax.random.uniform(ks[6], (DEPTH, DN_HEADS), f32, math_log(0.001), math_log(0.1)))
    dt_bias = dt + jnp.log(-jnp.expm1(-dt))
    dn_norm_g = 1.0 + 0.02 * jax.random.normal(ks[7], (DEPTH, DN_DV), f32)
    gla_w2 = jax.random.normal(ks[8], (DEPTH, GLA_RANK, GLA_QK), f32) * GLA_RANK ** -0.5
    gla_b = 0.01 * jax.random.normal(ks[9], (DEPTH, GLA_QK), f32)
    gla_norm_g = 1.0 + 0.02 * jax.random.normal(ks[10], (DEPTH, GLA_DV), f32)
    w_out = jax.random.normal(ks[11], (DEPTH, MIX_WIDTH, D_MODEL), f32) * MIX_WIDTH ** -0.5
    norm2_g = 1.0 + 0.02 * jax.random.normal(ks[12], (DEPTH, D_MODEL), f32)
    w_up = jax.random.normal(ks[13], (DEPTH, D_MODEL, D_FF), f32) * D_MODEL ** -0.5
    w_down = jax.random.normal(ks[14], (DEPTH, D_FF, D_MODEL), f32) * D_FF ** -0.5
    final_norm_g = 1.0 + 0.02 * jax.random.normal(ks[15], (D_MODEL,), f32)
    return {"x": x, "meta_tokens": meta_tokens, "norm1_g": norm1_g, "w_in": w_in, "conv_w": conv_w,
            "a_log": a_log, "dt_bias": dt_bias, "dn_norm_g": dn_norm_g, "gla_w2": gla_w2, "gla_b": gla_b,
            "gla_norm_g": gla_norm_g, "w_out": w_out, "norm2_g": norm2_g, "w_up": w_up, "w_down": w_down,
            "final_norm_g": final_norm_g}


def math_log(v):
    return float(np.log(v))


def reference(x, meta_tokens, norm1_g, w_in, conv_w, a_log, dt_bias, dn_norm_g, gla_w2, gla_b,
              gla_norm_g, w_out, norm2_g, w_up, w_down, final_norm_g):
    B = x.shape[0]
    pad = jnp.zeros((B, N_PAD, D_MODEL), x.dtype)
    meta = jnp.broadcast_to(meta_tokens.astype(x.dtype)[None], (B, N_META, D_MODEL))
    h = jnp.concatenate([pad, meta, x], axis=1)
    T = h.shape[1]
    valid = jnp.arange(T) >= N_PAD
    for l in range(DEPTH):
        h = hybrid_layer(h, valid, norm1_g[l], w_in[l], conv_w[l], a_log[l], dt_bias[l], dn_norm_g[l],
                         gla_w2[l], gla_b[l], gla_norm_g[l], w_out[l], norm2_g[l], w_up[l], w_down[l])
    return rmsnorm(h, final_norm_g)[:, CHUNK:]
```

```python
import functools

import jax
import jax.numpy as jnp
from jax import lax
from jax.experimental import pallas as pl
from jax.experimental.pallas import tpu as pltpu

F32 = jnp.float32
BF16 = jnp.bfloat16

D_MODEL = 1024
N_META = 16
CHUNK = 64
N_PAD = CHUNK - N_META
DN_HEADS = 4
DN_DK = 128
DN_DV = 128
CONV_K = 4
GLA_HEADS = 4
GLA_DK = 64
GLA_DV = 128
GLA_RANK = 16
GLA_NORMALIZER = 16.0
D_FF = 4 * D_MODEL
EPS = 1e-6

DN_QK = DN_HEADS * DN_DK
DN_V = DN_HEADS * DN_DV
GLA_QK = GLA_HEADS * GLA_DK
GLA_V = GLA_HEADS * GLA_DV
MIX_WIDTH = DN_V + GLA_V
CONV_W = 2 * DN_QK + DN_V

OFF_QKV = 0
OFF_Z = OFF_QKV + CONV_W
OFF_GQ = OFF_Z + DN_V
OFF_GK = OFF_GQ + GLA_QK
OFF_GV = OFF_GK + GLA_QK
OFF_GR = OFF_GV + GLA_V
BIG_W = OFF_GR + GLA_V
SM_W = 128
SM_BETA = 0
SM_A = DN_HEADS
SM_GLR = 2 * DN_HEADS
HIST = 8

V7X_VMEM_LIMIT = 56 * 1024 * 1024


def _mm(a, b):
    return jnp.dot(a, b, preferred_element_type=F32)


def _mm_nt(a, b):
    return lax.dot_general(a, b, (((1,), (1,)), ((), ())), preferred_element_type=F32)


def _mm_tn(a, b):
    return lax.dot_general(a, b, (((0,), (0,)), ((), ())), preferred_element_type=F32)


def _softplus(x):
    return jnp.maximum(x, 0.0) + jnp.log1p(jnp.exp(-jnp.abs(x)))


def _silu(x):
    return x * jax.nn.sigmoid(x)


def _cumsum_rows(x):
    row = lax.broadcasted_iota(jnp.int32, x.shape, 0)
    s = 1
    while s < x.shape[0]:
        x = x + jnp.where(row >= s, pltpu.roll(x, s, 0), 0.0)
        s *= 2
    return x


def _headwise(x, width, fn):
    outs = []
    for h in range(x.shape[1] // width):
        xs = x[:, h * width:(h + 1) * width]
        outs.append(xs * fn(jnp.sum(xs * xs, axis=-1, keepdims=True)))
    return jnp.concatenate(outs, axis=1)


def _bcast_heads(cols, width):
    c, n = cols[0].shape[0], len(cols)
    if width % 128 == 0:
        return jnp.concatenate([jnp.broadcast_to(col, (c, width)) for col in cols], axis=1)
    lane = lax.broadcasted_iota(jnp.int32, (c, n * width), 1)
    out = jnp.broadcast_to(cols[-1], (c, n * width))
    for h in reversed(range(n - 1)):
        out = jnp.where(lane < (h + 1) * width, jnp.broadcast_to(cols[h], (c, n * width)), out)
    return out


def _block_diag(x16, mask_ref):
    return jnp.concatenate([x16] * 4, axis=0) * mask_ref[...]


def _in_proj_kernel(n_pad, tm, x_ref, g_ref, wbig_ref, wsm_ref, big_ref, sm_ref):
    x = x_ref[...]
    h = x * lax.rsqrt(jnp.mean(x * x, axis=-1, keepdims=True) + EPS) * g_ref[...]
    if n_pad:
        row = pl.program_id(0) * tm + lax.broadcasted_iota(jnp.int32, h.shape, 0)
        h = jnp.where(row >= n_pad, h, 0.0)
    h16 = h.astype(BF16)
    nb = 512
    for j in range(BIG_W // nb):
        big_ref[:, j * nb:(j + 1) * nb] = _mm(h16, wbig_ref[:, j * nb:(j + 1) * nb]).astype(BF16)
    sm_ref[...] = _mm(h16, wsm_ref[...])


def _in_proj(x2d, norm_g, w_big, w_sm, n_pad):
    rows = x2d.shape[0]
    tm = min(512, rows)
    assert rows % tm == 0
    return pl.pallas_call(
        functools.partial(_in_proj_kernel, n_pad, tm),
        grid=(rows // tm,),
        in_specs=[
            pl.BlockSpec((tm, D_MODEL), lambda i: (i, 0)),
            pl.BlockSpec((1, D_MODEL), lambda i: (0, 0)),
            pl.BlockSpec((D_MODEL, BIG_W), lambda i: (0, 0)),
            pl.BlockSpec((D_MODEL, SM_W), lambda i: (0, 0)),
        ],
        out_specs=[
            pl.BlockSpec((tm, BIG_W), lambda i: (i, 0)),
            pl.BlockSpec((tm, SM_W), lambda i: (i, 0)),
        ],
        out_shape=[
            jax.ShapeDtypeStruct((rows, BIG_W), BF16),
            jax.ShapeDtypeStruct((rows, SM_W), F32),
        ],
        compiler_params=pltpu.CompilerParams(
            dimension_semantics=("arbitrary",), vmem_limit_bytes=V7X_VMEM_LIMIT),
        name="in_proj",
    )(x2d, norm_g, w_big, w_sm)


def _mixer_chunk(r0, row_base, n_pad, p_ref, sm_ref, hist_ref, cw, alog_ref, dtb_ref, dng_ref, glg_ref,
                 w2_ref, gb_ref, m512_ref, m256_ref, mpair_ref, mgla_ref, lowm_ref, sdn_ref, sgl_ref, mix_ref):
    C = CHUNK
    rows = slice(r0, r0 + C)
    tril = lowm_ref[0]
    strict = lowm_ref[1]
    eye = lowm_ref[2]

    sm = sm_ref[0, rows, :]
    beta_f = jax.nn.sigmoid(sm)
    g_f = -jnp.exp(alog_ref[...]) * _softplus(sm + dtb_ref[...])
    if n_pad:
        valid = lambda width: (row_base + r0 + lax.broadcasted_iota(jnp.int32, (C, width), 0)) >= n_pad
        g_f = jnp.where(valid(SM_W), g_f, 0.0)
    gc_f = _cumsum_rows(g_f)
    beta_cols = [beta_f[:, SM_BETA + h:SM_BETA + h + 1] for h in range(DN_HEADS)]
    g_cols = [g_f[:, SM_A + h:SM_A + h + 1] for h in range(DN_HEADS)]
    gc_cols = [gc_f[:, SM_A + h:SM_A + h + 1] for h in range(DN_HEADS)]
    beta64 = _bcast_heads(beta_cols, C)
    beta128 = _bcast_heads(beta_cols, DN_DK)
    gc128 = _bcast_heads(gc_cols, DN_DK)
    dlog = _cumsum_rows(_bcast_heads(g_cols, C) * strict)
    decay = jnp.exp(dlog) * tril
    gcl = gc128[C - 1:C, :]
    egc = jnp.exp(gc128)
    elast = jnp.exp(gcl)
    ekd = jnp.exp(gcl - gc128)

    y = hist_ref[HIST - 3 + r0:HIST - 3 + r0 + C, :] * cw[0:1, :]
    for i in range(1, CONV_K):
        y = y + hist_ref[HIST - 3 + r0 + i:HIST - 3 + r0 + i + C, :] * cw[i:i + 1, :]
    qkv = _silu(y)
    l2 = lambda s: lax.rsqrt(s + EPS)
    q = _headwise(qkv[:, :DN_QK], DN_DK, l2) * (DN_DK ** -0.5)
    k = _headwise(qkv[:, DN_QK:2 * DN_QK], DN_DK, l2)
    v = qkv[:, 2 * DN_QK:]

    kbd = _block_diag(k.astype(BF16), m512_ref)
    kkqk = _mm_nt(jnp.concatenate([k, q], axis=0).astype(BF16), kbd)
    lmat = kkqk[:C] * beta64 * decay * strict
    attn = kkqk[C:] * decay
    pj = -lmat
    t = eye + pj
    n_sq = CHUNK.bit_length() - 2
    for j in range(n_sq + 1):
        pbd = _block_diag(pj.astype(BF16), m256_ref)
        if j == 0:
            pj = _mm(pj.astype(BF16), pbd)
        elif j < n_sq:
            both = _mm(jnp.concatenate([pj, t], axis=0).astype(BF16), pbd)
            pj = both[:C]
            t = t + both[C:]
        else:
            t = t + _mm(t.astype(BF16), pbd)
    t16 = t.astype(BF16)
    u = _mm(t16, _block_diag((v * beta128).astype(BF16), m512_ref))
    w = _mm(t16, _block_diag((k * beta128 * egc).astype(BF16), m512_ref))

    qg = q * egc
    kd = k * ekd
    wq = jnp.concatenate([w, qg], axis=0).astype(BF16)
    ws, qs = [], []
    for p in range(2):
        res = _mm(wq[:, 256 * p:256 * (p + 1)], sdn_ref[0, p].astype(BF16))
        ws.append(res[:C])
        qs.append(res[C:])
    v_new = u - jnp.concatenate(ws, axis=1)
    vn16 = v_new.astype(BF16)
    o_dn = jnp.concatenate(qs, axis=1) + _mm(attn.astype(BF16), _block_diag(vn16, m512_ref))
    kd16 = kd.astype(BF16)
    for p in range(2):
        upd = _mm_tn(kd16[:, 256 * p:256 * (p + 1)], vn16[:, 256 * p:256 * (p + 1)])
        sdn_ref[0, p] = sdn_ref[0, p] * elast[:, 256 * p:256 * (p + 1)] + upd * mpair_ref[...]

    rms = lambda s: lax.rsqrt(s * (1.0 / DN_DV) + EPS)
    z = p_ref[0, rows, OFF_Z:OFF_Z + DN_V].astype(F32)
    mix_ref[0, rows, 0:DN_V] = (_headwise(o_dn, DN_DV, rms) * dng_ref[...] * _silu(z)).astype(BF16)

    gq = p_ref[0, rows, OFF_GQ:OFF_GQ + GLA_QK].astype(F32) * (GLA_DK ** -0.5)
    gk = p_ref[0, rows, OFF_GK:OFF_GK + GLA_QK].astype(F32)
    gv16 = p_ref[0, rows, OFF_GV:OFF_GV + GLA_V]
    gr = p_ref[0, rows, OFF_GR:OFF_GR + GLA_V].astype(F32)
    zg = _mm(sm.astype(BF16), w2_ref[...]) + gb_ref[...]
    gl = -_softplus(-zg) * (1.0 / GLA_NORMALIZER)
    if n_pad:
        gl = jnp.where(valid(GLA_QK), gl, 0.0)
    b = _cumsum_rows(gl)
    bref = b[C // 2:C // 2 + 1, :]
    blast = b[C - 1:C, :]
    qi = gq * jnp.exp(b - bref)
    ki = gk * jnp.exp(bref - b)
    a_gla = _mm_nt(qi.astype(BF16), _block_diag(ki.astype(BF16), m256_ref)) * tril
    qg_g = gq * jnp.exp(b)
    kd_g = gk * jnp.exp(blast - b)
    o_gla = _mm(a_gla.astype(BF16), _block_diag(gv16, m512_ref)) + _mm_nt(qg_g.astype(BF16), sgl_ref[0].astype(BF16))
    sgl_ref[0] = sgl_ref[0] * jnp.exp(blast) + _mm_tn(gv16, kd_g.astype(BF16)) * mgla_ref[...]
    rms_g = lambda s: lax.rsqrt(s * (1.0 / GLA_DV) + EPS)
    mix_ref[0, rows, DN_V:MIX_WIDTH] = (_headwise(o_gla, GLA_DV, rms_g) * glg_ref[...] * _silu(gr)).astype(BF16)


def _mixer_kernel(n_pad, R, p_ref, sm_ref, cw_ref, alog_ref, dtb_ref, dng_ref, glg_ref, w2_ref, gb_ref,
                  m512_ref, m256_ref, mpair_ref, mgla_ref, lowm_ref, sdn0_ref, sgl0_ref, tail0_ref,
                  mix_ref, sdn_ref, sgl_ref, tail_ref, hist_ref):
    i = pl.program_id(1)

    @pl.when(i == 0)
    def _():
        hist_ref[0:HIST, :] = tail0_ref[0]
        sdn_ref[...] = sdn0_ref[...]
        sgl_ref[...] = sgl0_ref[...]

    hist_ref[HIST:HIST + R, :] = p_ref[0, :, OFF_QKV:OFF_QKV + CONV_W].astype(F32)
    cw = cw_ref[...]
    for n in range(R // CHUNK):
        _mixer_chunk(n * CHUNK, i * R, n_pad, p_ref, sm_ref, hist_ref, cw, alog_ref, dtb_ref, dng_ref, glg_ref,
                     w2_ref, gb_ref, m512_ref, m256_ref, mpair_ref, mgla_ref, lowm_ref, sdn_ref, sgl_ref, mix_ref)
    tail = hist_ref[R:R + HIST, :]
    hist_ref[0:HIST, :] = tail
    tail_ref[0] = tail


def _mixer_masks():
    r = jnp.arange(4 * CHUNK)[:, None]
    m512 = (r // CHUNK == jnp.arange(DN_V)[None, :] // DN_DV).astype(BF16)
    m256 = (r // CHUNK == jnp.arange(4 * CHUNK)[None, :] // CHUNK).astype(BF16)
    mpair = (r // DN_DK == jnp.arange(2 * DN_DV)[None, :] // DN_DV).astype(F32)
    mgla = (jnp.arange(GLA_V)[:, None] // GLA_DV == jnp.arange(GLA_QK)[None, :] // GLA_DK).astype(F32)
    c = jnp.arange(CHUNK)[:, None]
    s = jnp.arange(4 * CHUNK)[None, :] % CHUNK
    lowm = jnp.stack([c >= s, c > s, c == s]).astype(F32)
    return m512, m256, mpair, mgla, lowm


def _mixer(big, sm, consts, sdn0, sgl0, tail0, n_pad):
    B, T, _ = big.shape
    R = min(256, T)
    assert T % R == 0 and R % CHUNK == 0
    masks = _mixer_masks()
    const_arrays = list(consts) + list(masks)
    full = lambda a: pl.BlockSpec(a.shape, lambda b, i: (0,) * a.ndim)
    return pl.pallas_call(
        functools.partial(_mixer_kernel, n_pad, R),
        grid=(B, T // R),
        in_specs=[
            pl.BlockSpec((1, R, BIG_W), lambda b, i: (b, i, 0)),
            pl.BlockSpec((1, R, SM_W), lambda b, i: (b, i, 0)),
        ] + [full(a) for a in const_arrays] + [full(sdn0), full(sgl0), full(tail0)],
        out_specs=[
            pl.BlockSpec((1, R, MIX_WIDTH), lambda b, i: (b, i, 0)),
            pl.BlockSpec((1, 2, 2 * DN_DK, 2 * DN_DV), lambda b, i: (b, 0, 0, 0)),
            pl.BlockSpec((1, GLA_V, GLA_QK), lambda b, i: (b, 0, 0)),
            pl.BlockSpec((1, HIST, CONV_W), lambda b, i: (b, 0, 0)),
        ],
        out_shape=[
            jax.ShapeDtypeStruct((B, T, MIX_WIDTH), BF16),
            jax.ShapeDtypeStruct((B, 2, 2 * DN_DK, 2 * DN_DV), F32),
            jax.ShapeDtypeStruct((B, GLA_V, GLA_QK), F32),
            jax.ShapeDtypeStruct((B, HIST, CONV_W), F32),
        ],
        scratch_shapes=[pltpu.VMEM((HIST + R, CONV_W), F32)],
        compiler_params=pltpu.CompilerParams(
            dimension_semantics=("arbitrary", "arbitrary"), vmem_limit_bytes=V7X_VMEM_LIMIT),
        name="mixer",
    )(big, sm, *const_arrays, sdn0, sgl0, tail0)


def _out_mlp_kernel(x_ref, mix_ref, wout_ref, g2_ref, wup_ref, wdown_ref, gf_ref, o_ref):
    x1 = x_ref[...] + _mm(mix_ref[...], wout_ref[...])
    h2 = (x1 * lax.rsqrt(jnp.mean(x1 * x1, axis=-1, keepdims=True) + EPS) * g2_ref[...]).astype(BF16)
    acc = x1
    nb = 1024
    for j in range(D_FF // nb):
        a = jnp.maximum(_mm(h2, wup_ref[:, j * nb:(j + 1) * nb]), 0.0)
        acc = acc + _mm((a * a).astype(BF16), wdown_ref[j * nb:(j + 1) * nb, :])
    o_ref[...] = acc * lax.rsqrt(jnp.mean(acc * acc, axis=-1, keepdims=True) + EPS) * gf_ref[...]


def _out_mlp(x2d, mix2d, w_out, norm2_g, w_up, w_down, final_g):
    rows = x2d.shape[0]
    tm = min(512, rows)
    assert rows % tm == 0
    const = lambda a: pl.BlockSpec(a.shape, lambda i: (0, 0), pipeline_mode=pl.Buffered(1))
    return pl.pallas_call(
        _out_mlp_kernel,
        grid=(rows // tm,),
        in_specs=[
            pl.BlockSpec((tm, D_MODEL), lambda i: (i, 0)),
            pl.BlockSpec((tm, MIX_WIDTH), lambda i: (i, 0)),
            const(w_out), const(norm2_g), const(w_up), const(w_down), const(final_g),
        ],
        out_specs=pl.BlockSpec((tm, D_MODEL), lambda i: (i, 0)),
        out_shape=jax.ShapeDtypeStruct((rows, D_MODEL), F32),
        compiler_params=pltpu.CompilerParams(
            dimension_semantics=("arbitrary",), vmem_limit_bytes=V7X_VMEM_LIMIT),
        name="out_mlp",
    )(x2d, mix2d, w_out, norm2_g, w_up, w_down, final_g)


def _pad_lanes(row, offset, width):
    return jnp.zeros((1, width), F32).at[0, offset:offset + row.shape[0]].set(row.astype(F32))


def kernel(x, meta_tokens, norm1_g, w_in, conv_w, a_log, dt_bias, dn_norm_g, gla_w2, gla_b, gla_norm_g,
           w_out, norm2_g, w_up, w_down, final_norm_g):
    B, S, D = x.shape
    assert norm1_g.shape[0] == 1 and D == D_MODEL and S % CHUNK == 0

    offs = [0]
    for width in (DN_QK, DN_QK, DN_V, DN_V, DN_HEADS, DN_HEADS, GLA_QK, GLA_QK, GLA_V, GLA_V, GLA_RANK):
        offs.append(offs[-1] + width)
    w = w_in[0]
    col = lambda i: w[:, offs[i]:offs[i + 1]]
    w_big = jnp.concatenate([col(0), col(1), col(2), col(3), col(6), col(7), col(8), col(9)], axis=1).astype(BF16)
    w_sm = jnp.concatenate(
        [col(4), col(5), col(10), jnp.zeros((D, SM_W - 2 * DN_HEADS - GLA_RANK), w.dtype)], axis=1).astype(BF16)

    w2_pad = jnp.zeros((SM_W, GLA_QK), F32).at[SM_GLR:SM_GLR + GLA_RANK].set(gla_w2[0]).astype(BF16)
    consts = (
        conv_w[0].astype(F32),
        _pad_lanes(a_log[0], SM_A, SM_W),
        _pad_lanes(dt_bias[0], SM_A, SM_W),
        jnp.tile(dn_norm_g[0].astype(F32), DN_HEADS)[None, :],
        jnp.tile(gla_norm_g[0].astype(F32), GLA_HEADS)[None, :],
        w2_pad,
        gla_b[0].astype(F32)[None, :],
    )
    g1 = norm1_g[0].astype(F32)[None, :]

    prefix = jnp.concatenate([jnp.zeros((N_PAD, D), x.dtype), meta_tokens.astype(x.dtype)], axis=0)
    big0, sm0 = _in_proj(prefix, g1, w_big, w_sm, N_PAD)
    zeros = lambda *shape: jnp.zeros(shape, F32)
    _, sdn0, sgl0, tail0 = _mixer(
        big0[None], sm0[None], consts,
        zeros(1, 2, 2 * DN_DK, 2 * DN_DV), zeros(1, GLA_V, GLA_QK), zeros(1, HIST, CONV_W), N_PAD)

    x2d = x.reshape(B * S, D)
    big, sm = _in_proj(x2d, g1, w_big, w_sm, 0)
    mix, _, _, _ = _mixer(big.reshape(B, S, BIG_W), sm.reshape(B, S, SM_W), consts, sdn0, sgl0, tail0, 0)
    out = _out_mlp(x2d, mix.reshape(B * S, MIX_WIDTH), w_out[0].astype(BF16), norm2_g[0].astype(F32)[None, :],
                   w_up[0].astype(BF16), w_down[0].astype(BF16), final_norm_g.astype(F32)[None, :])
    return out.reshape(B, S, D)
```

```python
import functools

import jax
import jax.numpy as jnp
from jax import lax
from jax.experimental import pallas as pl
from jax.experimental.pallas import tpu as pltpu

F32 = jnp.float32
BF16 = jnp.bfloat16

D_MODEL = 1024
N_META = 16
CHUNK = 64
N_PAD = CHUNK - N_META
DN_HEADS = 4
DN_DK = 128
DN_DV = 128
CONV_K = 4
GLA_HEADS = 4
GLA_DK = 64
GLA_DV = 128
GLA_RANK = 16
GLA_NORMALIZER = 16.0
D_FF = 4 * D_MODEL
EPS = 1e-6

DN_QK = DN_HEADS * DN_DK
DN_V = DN_HEADS * DN_DV
GLA_QK = GLA_HEADS * GLA_DK
GLA_V = GLA_HEADS * GLA_DV
MIX_WIDTH = DN_V + GLA_V
CONV_W = 2 * DN_QK + DN_V

OFF_QKV = 0
OFF_Z = OFF_QKV + CONV_W
OFF_GQ = OFF_Z + DN_V
OFF_GK = OFF_GQ + GLA_QK
OFF_GV = OFF_GK + GLA_QK
OFF_GR = OFF_GV + GLA_V
BIG_W = OFF_GR + GLA_V
SM_W = 128
SM_BETA = 0
SM_A = DN_HEADS
SM_GLR = 2 * DN_HEADS
HIST = 8

ROW_TILE = 512
MIXER_ROWS = 256
V7X_VMEM_LIMIT = 56 * 1024 * 1024


def _mm(a, b):
    return jnp.dot(a, b, preferred_element_type=F32)


def _mm_nt(a, b):
    return lax.dot_general(a, b, (((1,), (1,)), ((), ())), preferred_element_type=F32)


def _mm_tn(a, b):
    return lax.dot_general(a, b, (((0,), (0,)), ((), ())), preferred_element_type=F32)


def _softplus(x):
    return jnp.maximum(x, 0.0) + jnp.log1p(jnp.exp(-jnp.abs(x)))


def _silu(x):
    return x * jax.nn.sigmoid(x)


def _cumsum_rows(x):
    row = lax.broadcasted_iota(jnp.int32, x.shape, 0)
    s = 1
    while s < x.shape[0]:
        x = x + jnp.where(row >= s, pltpu.roll(x, s, 0), 0.0)
        s *= 2
    return x


def _headwise(x, width, fn):
    outs = []
    for h in range(x.shape[1] // width):
        xs = x[:, h * width:(h + 1) * width]
        outs.append(xs * fn(jnp.sum(xs * xs, axis=-1, keepdims=True)))
    return jnp.concatenate(outs, axis=1)


def _bcast_heads(cols, width):
    c, n = cols[0].shape[0], len(cols)
    if width % 128 == 0:
        return jnp.concatenate([jnp.broadcast_to(col, (c, width)) for col in cols], axis=1)
    lane = lax.broadcasted_iota(jnp.int32, (c, n * width), 1)
    out = jnp.broadcast_to(cols[-1], (c, n * width))
    for h in reversed(range(n - 1)):
        out = jnp.where(lane < (h + 1) * width, jnp.broadcast_to(cols[h], (c, n * width)), out)
    return out


def _block_diag(x16, mask_ref):
    return jnp.concatenate([x16] * 4, axis=0) * mask_ref[...]


def _in_proj_kernel(n_pad, tm, x_ref, g_ref, wbig_ref, wsm_ref, big_ref, sm_ref):
    x = x_ref[...]
    h = x * lax.rsqrt(jnp.mean(x * x, axis=-1, keepdims=True) + EPS) * g_ref[...]
    if n_pad:
        row = pl.program_id(0) * tm + lax.broadcasted_iota(jnp.int32, h.shape, 0)
        h = jnp.where(row >= n_pad, h, 0.0)
    h16 = h.astype(BF16)
    nb = 512
    for j in range(BIG_W // nb):
        big_ref[:, j * nb:(j + 1) * nb] = _mm(h16, wbig_ref[:, j * nb:(j + 1) * nb]).astype(BF16)
    sm_ref[...] = _mm(h16, wsm_ref[...])


def _in_proj(x2d, norm_g, w_big, w_sm, n_pad):
    rows = x2d.shape[0]
    tm = min(ROW_TILE, rows)
    assert rows % tm == 0
    return pl.pallas_call(
        functools.partial(_in_proj_kernel, n_pad, tm),
        grid=(rows // tm,),
        in_specs=[
            pl.BlockSpec((tm, D_MODEL), lambda i: (i, 0)),
            pl.BlockSpec((1, D_MODEL), lambda i: (0, 0)),
            pl.BlockSpec((D_MODEL, BIG_W), lambda i: (0, 0)),
            pl.BlockSpec((D_MODEL, SM_W), lambda i: (0, 0)),
        ],
        out_specs=[
            pl.BlockSpec((tm, BIG_W), lambda i: (i, 0)),
            pl.BlockSpec((tm, SM_W), lambda i: (i, 0)),
        ],
        out_shape=[
            jax.ShapeDtypeStruct((rows, BIG_W), BF16),
            jax.ShapeDtypeStruct((rows, SM_W), F32),
        ],
        compiler_params=pltpu.CompilerParams(
            dimension_semantics=("arbitrary",), vmem_limit_bytes=V7X_VMEM_LIMIT),
        name="in_proj",
    )(x2d, norm_g, w_big, w_sm)


def _mixer_front(r0, row_base, n_pad, p_ref, sm_ref, hist_ref, cw, alog_ref, dtb_ref, w2_ref, gb_ref,
                 m512_ref, m256_ref, lowm_ref):
    C = CHUNK
    rows = slice(r0, r0 + C)
    tril = lowm_ref[0]
    strict = lowm_ref[1]
    eye = lowm_ref[2]
    st = {}

    sm = sm_ref[0, rows, :]
    beta_f = jax.nn.sigmoid(sm)
    g_f = -jnp.exp(alog_ref[...]) * _softplus(sm + dtb_ref[...])
    if n_pad:
        valid = lambda width: (row_base + r0 + lax.broadcasted_iota(jnp.int32, (C, width), 0)) >= n_pad
        g_f = jnp.where(valid(SM_W), g_f, 0.0)
    gc_f = _cumsum_rows(g_f)
    beta_cols = [beta_f[:, SM_BETA + h:SM_BETA + h + 1] for h in range(DN_HEADS)]
    g_cols = [g_f[:, SM_A + h:SM_A + h + 1] for h in range(DN_HEADS)]
    gc_cols = [gc_f[:, SM_A + h:SM_A + h + 1] for h in range(DN_HEADS)]
    beta64 = _bcast_heads(beta_cols, C)
    beta128 = _bcast_heads(beta_cols, DN_DK)
    gc128 = _bcast_heads(gc_cols, DN_DK)
    dlog = _cumsum_rows(_bcast_heads(g_cols, C) * strict)
    decay = jnp.exp(dlog) * tril
    gcl = gc128[C - 1:C, :]
    egc = jnp.exp(gc128)
    st["elast"] = jnp.exp(gcl)

    y = hist_ref[HIST - 3 + r0:HIST - 3 + r0 + C, :] * cw[0:1, :]
    for i in range(1, CONV_K):
        y = y + hist_ref[HIST - 3 + r0 + i:HIST - 3 + r0 + i + C, :] * cw[i:i + 1, :]
    qkv = _silu(y)
    l2 = lambda s: lax.rsqrt(s + EPS)
    q = _headwise(qkv[:, :DN_QK], DN_DK, l2) * (DN_DK ** -0.5)
    k = _headwise(qkv[:, DN_QK:2 * DN_QK], DN_DK, l2)
    v = qkv[:, 2 * DN_QK:]

    kbd = _block_diag(k.astype(BF16), m512_ref)
    kkqk = _mm_nt(jnp.concatenate([k, q], axis=0).astype(BF16), kbd)
    lmat = kkqk[:C] * beta64 * decay * strict
    st["attn16"] = (kkqk[C:] * decay).astype(BF16)
    st["pj"] = -lmat
    st["t"] = eye - lmat
    st["vb16"] = (v * beta128).astype(BF16)
    st["kbg16"] = (k * beta128 * egc).astype(BF16)
    st["qg16"] = (q * egc).astype(BF16)
    st["kd16"] = (k * jnp.exp(gcl - gc128)).astype(BF16)

    gq = p_ref[0, rows, OFF_GQ:OFF_GQ + GLA_QK].astype(F32) * (GLA_DK ** -0.5)
    gk = p_ref[0, rows, OFF_GK:OFF_GK + GLA_QK].astype(F32)
    gv16 = p_ref[0, rows, OFF_GV:OFF_GV + GLA_V]
    zg = _mm(sm.astype(BF16), w2_ref[...]) + gb_ref[...]
    gl = -_softplus(-zg) * (1.0 / GLA_NORMALIZER)
    if n_pad:
        gl = jnp.where(valid(GLA_QK), gl, 0.0)
    b = _cumsum_rows(gl)
    bref = b[C // 2:C // 2 + 1, :]
    blast = b[C - 1:C, :]
    qi = gq * jnp.exp(b - bref)
    ki = gk * jnp.exp(bref - b)
    a_gla = _mm_nt(qi.astype(BF16), _block_diag(ki.astype(BF16), m256_ref)) * tril
    st["o_gla_intra"] = _mm(a_gla.astype(BF16), _block_diag(gv16, m512_ref))
    st["qg_g16"] = (gq * jnp.exp(b)).astype(BF16)
    st["kd_g16"] = (gk * jnp.exp(blast - b)).astype(BF16)
    st["glast"] = jnp.exp(blast)
    return st


def _inverse_step(j, n_sq, st, m256_ref):
    C = CHUNK
    pbd = _block_diag(st["pj"].astype(BF16), m256_ref)
    if j == 0:
        st["pj"] = _mm(st["pj"].astype(BF16), pbd)
    elif j < n_sq:
        both = _mm(jnp.concatenate([st["pj"], st["t"]], axis=0).astype(BF16), pbd)
        st["pj"] = both[:C]
        st["t"] = st["t"] + both[C:]
    else:
        st["t"] = st["t"] + _mm(st["t"].astype(BF16), pbd)


def _mixer_recur(r0, st, p_ref, dng_ref, glg_ref, m512_ref, mpair_ref, mgla_ref, sdn_ref, sgl_ref, mix_ref):
    C = CHUNK
    rows = slice(r0, r0 + C)
    wq = jnp.concatenate([st["w"].astype(BF16), st["qg16"]], axis=0)
    ws, qs = [], []
    for p in range(2):
        res = _mm(wq[:, 256 * p:256 * (p + 1)], sdn_ref[0, p].astype(BF16))
        ws.append(res[:C])
        qs.append(res[C:])
    gv16 = p_ref[0, rows, OFF_GV:OFF_GV + GLA_V]
    o_gla = st["o_gla_intra"] + _mm_nt(st["qg_g16"], sgl_ref[0].astype(BF16))
    sgl_ref[0] = sgl_ref[0] * st["glast"] + _mm_tn(gv16, st["kd_g16"]) * mgla_ref[...]

    v_new = st["u"] - jnp.concatenate(ws, axis=1)
    vn16 = v_new.astype(BF16)
    o_dn = jnp.concatenate(qs, axis=1) + _mm(st["attn16"], _block_diag(vn16, m512_ref))
    for p in range(2):
        upd = _mm_tn(st["kd16"][:, 256 * p:256 * (p + 1)], vn16[:, 256 * p:256 * (p + 1)])
        sdn_ref[0, p] = sdn_ref[0, p] * st["elast"][:, 256 * p:256 * (p + 1)] + upd * mpair_ref[...]

    rms = lambda s: lax.rsqrt(s * (1.0 / DN_DV) + EPS)
    z = p_ref[0, rows, OFF_Z:OFF_Z + DN_V].astype(F32)
    mix_ref[0, rows, 0:DN_V] = (_headwise(o_dn, DN_DV, rms) * dng_ref[...] * _silu(z)).astype(BF16)
    gr = p_ref[0, rows, OFF_GR:OFF_GR + GLA_V].astype(F32)
    rms_g = lambda s: lax.rsqrt(s * (1.0 / GLA_DV) + EPS)
    mix_ref[0, rows, DN_V:MIX_WIDTH] = (_headwise(o_gla, GLA_DV, rms_g) * glg_ref[...] * _silu(gr)).astype(BF16)


def _mixer_kernel(n_pad, R, p_ref, sm_ref, cw_ref, alog_ref, dtb_ref, dng_ref, glg_ref, w2_ref, gb_ref,
                  m512_ref, m256_ref, mpair_ref, mgla_ref, lowm_ref, sdn0_ref, sgl0_ref, tail0_ref,
                  mix_ref, sdn_ref, sgl_ref, tail_ref, hist_ref):
    i = pl.program_id(1)

    @pl.when(i == 0)
    def _():
        hist_ref[0:HIST, :] = tail0_ref[0]
        sdn_ref[...] = sdn0_ref[...]
        sgl_ref[...] = sgl0_ref[...]

    hist_ref[HIST:HIST + R, :] = p_ref[0, :, OFF_QKV:OFF_QKV + CONV_W].astype(F32)
    cw = cw_ref[...]
    sts = [_mixer_front(n * CHUNK, i * R, n_pad, p_ref, sm_ref, hist_ref, cw, alog_ref, dtb_ref, w2_ref, gb_ref,
                        m512_ref, m256_ref, lowm_ref) for n in range(R // CHUNK)]
    n_sq = CHUNK.bit_length() - 2
    for j in range(n_sq + 1):
        for st in sts:
            _inverse_step(j, n_sq, st, m256_ref)
    for st in sts:
        t16 = st["t"].astype(BF16)
        st["u"] = _mm(t16, _block_diag(st["vb16"], m512_ref))
        st["w"] = _mm(t16, _block_diag(st["kbg16"], m512_ref))
    for n, st in enumerate(sts):
        _mixer_recur(n * CHUNK, st, p_ref, dng_ref, glg_ref, m512_ref, mpair_ref, mgla_ref, sdn_ref, sgl_ref, mix_ref)
    tail = hist_ref[R:R + HIST, :]
    hist_ref[0:HIST, :] = tail
    tail_ref[0] = tail


def _mixer_masks():
    r = jnp.arange(4 * CHUNK)[:, None]
    m512 = (r // CHUNK == jnp.arange(DN_V)[None, :] // DN_DV).astype(BF16)
    m256 = (r // CHUNK == jnp.arange(4 * CHUNK)[None, :] // CHUNK).astype(BF16)
    mpair = (r // DN_DK == jnp.arange(2 * DN_DV)[None, :] // DN_DV).astype(F32)
    mgla = (jnp.arange(GLA_V)[:, None] // GLA_DV == jnp.arange(GLA_QK)[None, :] // GLA_DK).astype(F32)
    c = jnp.arange(CHUNK)[:, None]
    s = jnp.arange(4 * CHUNK)[None, :] % CHUNK
    lowm = jnp.stack([c >= s, c > s, c == s]).astype(F32)
    return m512, m256, mpair, mgla, lowm


def _mixer(big, sm, consts, sdn0, sgl0, tail0, n_pad):
    B, T, _ = big.shape
    R = min(MIXER_ROWS, T)
    assert T % R == 0 and R % CHUNK == 0
    masks = _mixer_masks()
    const_arrays = list(consts) + list(masks)
    full = lambda a: pl.BlockSpec(a.shape, lambda b, i: (0,) * a.ndim)
    return pl.pallas_call(
        functools.partial(_mixer_kernel, n_pad, R),
        grid=(B, T // R),
        in_specs=[
            pl.BlockSpec((1, R, BIG_W), lambda b, i: (b, i, 0)),
            pl.BlockSpec((1, R, SM_W), lambda b, i: (b, i, 0)),
        ] + [full(a) for a in const_arrays] + [full(sdn0), full(sgl0), full(tail0)],
        out_specs=[
            pl.BlockSpec((1, R, MIX_WIDTH), lambda b, i: (b, i, 0)),
            pl.BlockSpec((1, 2, 2 * DN_DK, 2 * DN_DV), lambda b, i: (b, 0, 0, 0)),
            pl.BlockSpec((1, GLA_V, GLA_QK), lambda b, i: (b, 0, 0)),
            pl.BlockSpec((1, HIST, CONV_W), lambda b, i: (b, 0, 0)),
        ],
        out_shape=[
            jax.ShapeDtypeStruct((B, T, MIX_WIDTH), BF16),
            jax.ShapeDtypeStruct((B, 2, 2 * DN_DK, 2 * DN_DV), F32),
            jax.ShapeDtypeStruct((B, GLA_V, GLA_QK), F32),
            jax.ShapeDtypeStruct((B, HIST, CONV_W), F32),
        ],
        scratch_shapes=[pltpu.VMEM((HIST + R, CONV_W), F32)],
        compiler_params=pltpu.CompilerParams(
            dimension_semantics=("arbitrary", "arbitrary"), vmem_limit_bytes=V7X_VMEM_LIMIT),
        name="mixer",
    )(big, sm, *const_arrays, sdn0, sgl0, tail0)


def _out_mlp_kernel(x_ref, mix_ref, wout_ref, g2_ref, wup_ref, wdown_ref, gf_ref, o_ref):
    x1 = x_ref[...] + _mm(mix_ref[...], wout_ref[...])
    h2 = (x1 * lax.rsqrt(jnp.mean(x1 * x1, axis=-1, keepdims=True) + EPS) * g2_ref[...]).astype(BF16)
    acc = x1
    nb = 1024
    for j in range(D_FF // nb):
        a = jnp.maximum(_mm(h2, wup_ref[:, j * nb:(j + 1) * nb]), 0.0)
        acc = acc + _mm((a * a).astype(BF16), wdown_ref[j * nb:(j + 1) * nb, :])
    o_ref[...] = acc * lax.rsqrt(jnp.mean(acc * acc, axis=-1, keepdims=True) + EPS) * gf_ref[...]


def _out_mlp(x2d, mix2d, w_out, norm2_g, w_up, w_down, final_g):
    rows = x2d.shape[0]
    tm = min(ROW_TILE, rows)
    assert rows % tm == 0
    const = lambda a: pl.BlockSpec(a.shape, lambda i: (0, 0), pipeline_mode=pl.Buffered(1))
    return pl.pallas_call(
        _out_mlp_kernel,
        grid=(rows // tm,),
        in_specs=[
            pl.BlockSpec((tm, D_MODEL), lambda i: (i, 0)),
            pl.BlockSpec((tm, MIX_WIDTH), lambda i: (i, 0)),
            const(w_out), const(norm2_g), const(w_up), const(w_down), const(final_g),
        ],
        out_specs=pl.BlockSpec((tm, D_MODEL), lambda i: (i, 0)),
        out_shape=jax.ShapeDtypeStruct((rows, D_MODEL), F32),
        compiler_params=pltpu.CompilerParams(
            dimension_semantics=("arbitrary",), vmem_limit_bytes=V7X_VMEM_LIMIT),
        name="out_mlp",
    )(x2d, mix2d, w_out, norm2_g, w_up, w_down, final_g)


def _pad_lanes(row, offset, width):
    return jnp.zeros((1, width), F32).at[0, offset:offset + row.shape[0]].set(row.astype(F32))


def kernel(x, meta_tokens, norm1_g, w_in, conv_w, a_log, dt_bias, dn_norm_g, gla_w2, gla_b, gla_norm_g,
           w_out, norm2_g, w_up, w_down, final_norm_g):
    B, S, D = x.shape
    assert norm1_g.shape[0] == 1 and D == D_MODEL and S % CHUNK == 0

    offs = [0]
    for width in (DN_QK, DN_QK, DN_V, DN_V, DN_HEADS, DN_HEADS, GLA_QK, GLA_QK, GLA_V, GLA_V, GLA_RANK):
        offs.append(offs[-1] + width)
    w = w_in[0]
    col = lambda i: w[:, offs[i]:offs[i + 1]]
    w_big = jnp.concatenate([col(0), col(1), col(2), col(3), col(6), col(7), col(8), col(9)], axis=1).astype(BF16)
    w_sm = jnp.concatenate(
        [col(4), col(5), col(10), jnp.zeros((D, SM_W - 2 * DN_HEADS - GLA_RANK), w.dtype)], axis=1).astype(BF16)

    w2_pad = jnp.zeros((SM_W, GLA_QK), F32).at[SM_GLR:SM_GLR + GLA_RANK].set(gla_w2[0]).astype(BF16)
    consts = (
        conv_w[0].astype(F32),
        _pad_lanes(a_log[0], SM_A, SM_W),
        _pad_lanes(dt_bias[0], SM_A, SM_W),
        jnp.tile(dn_norm_g[0].astype(F32), DN_HEADS)[None, :],
        jnp.tile(gla_norm_g[0].astype(F32), GLA_HEADS)[None, :],
        w2_pad,
        gla_b[0].astype(F32)[None, :],
    )
    g1 = norm1_g[0].astype(F32)[None, :]

    prefix = jnp.concatenate([jnp.zeros((N_PAD, D), x.dtype), meta_tokens.astype(x.dtype)], axis=0)
    big0, sm0 = _in_proj(prefix, g1, w_big, w_sm, N_PAD)
    zeros = lambda *shape: jnp.zeros(shape, F32)
    _, sdn0, sgl0, tail0 = _mixer(
        big0[None], sm0[None], consts,
        zeros(1, 2, 2 * DN_DK, 2 * DN_DV), zeros(1, GLA_V, GLA_QK), zeros(1, HIST, CONV_W), N_PAD)

    x2d = x.reshape(B * S, D)
    big, sm = _in_proj(x2d, g1, w_big, w_sm, 0)
    mix, _, _, _ = _mixer(big.reshape(B, S, BIG_W), sm.reshape(B, S, SM_W), consts, sdn0, sgl0, tail0, 0)
    out = _out_mlp(x2d, mix.reshape(B * S, MIX_WIDTH), w_out[0].astype(BF16), norm2_g[0].astype(F32)[None, :],
                   w_up[0].astype(BF16), w_down[0].astype(BF16), final_norm_g.astype(F32)[None, :])
    return out.reshape(B, S, D)
```

```python
import functools

import jax
import jax.numpy as jnp
from jax import lax
from jax.experimental import pallas as pl
from jax.experimental.pallas import tpu as pltpu

F32 = jnp.float32
BF16 = jnp.bfloat16

D_MODEL = 1024
N_META = 16
CHUNK = 64
N_PAD = CHUNK - N_META
DN_HEADS = 4
DN_DK = 128
DN_DV = 128
CONV_K = 4
GLA_HEADS = 4
GLA_DK = 64
GLA_DV = 128
GLA_RANK = 16
GLA_NORMALIZER = 16.0
D_FF = 4 * D_MODEL
EPS = 1e-6

DN_QK = DN_HEADS * DN_DK
DN_V = DN_HEADS * DN_DV
GLA_QK = GLA_HEADS * GLA_DK
GLA_V = GLA_HEADS * GLA_DV
MIX_WIDTH = DN_V + GLA_V
CONV_W = 2 * DN_QK + DN_V

OFF_QKV = 0
OFF_Z = OFF_QKV + CONV_W
OFF_GQ = OFF_Z + DN_V
OFF_GK = OFF_GQ + GLA_QK
OFF_GV = OFF_GK + GLA_QK
OFF_GR = OFF_GV + GLA_V
BIG_W = OFF_GR + GLA_V
SM_W = 128
SM_BETA = 0
SM_A = DN_HEADS
SM_GLR = 2 * DN_HEADS

ROW_TILE = 512
MIXER_BATCH = 4
V7X_VMEM_LIMIT = 56 * 1024 * 1024


def _mm(a, b):
    return jnp.dot(a, b, preferred_element_type=F32)


def _mm_nt(a, b):
    return lax.dot_general(a, b, (((1,), (1,)), ((), ())), preferred_element_type=F32)


def _mm_tn(a, b):
    return lax.dot_general(a, b, (((0,), (0,)), ((), ())), preferred_element_type=F32)


def _softplus(x):
    return jnp.maximum(x, 0.0) + jnp.log1p(jnp.exp(-jnp.abs(x)))


def _silu(x):
    return x * jax.nn.sigmoid(x)


def _cumsum_rows(x):
    row = lax.broadcasted_iota(jnp.int32, x.shape, 0)
    s = 1
    while s < x.shape[0]:
        x = x + jnp.where(row >= s, pltpu.roll(x, s, 0), 0.0)
        s *= 2
    return x


def _headwise(x, width, fn):
    outs = []
    for h in range(x.shape[1] // width):
        xs = x[:, h * width:(h + 1) * width]
        outs.append(xs * fn(jnp.sum(xs * xs, axis=-1, keepdims=True)))
    return jnp.concatenate(outs, axis=1)


def _bcast_heads(cols, width):
    c, n = cols[0].shape[0], len(cols)
    if width % 128 == 0:
        return jnp.concatenate([jnp.broadcast_to(col, (c, width)) for col in cols], axis=1)
    lane = lax.broadcasted_iota(jnp.int32, (c, n * width), 1)
    out = jnp.broadcast_to(cols[-1], (c, n * width))
    for h in reversed(range(n - 1)):
        out = jnp.where(lane < (h + 1) * width, jnp.broadcast_to(cols[h], (c, n * width)), out)
    return out


def _block_diag(x16, mask_ref):
    return jnp.concatenate([x16] * 4, axis=0) * mask_ref[...]


def _in_proj_kernel(n_pad, tm, x_ref, g_ref, wbig_ref, wsm_ref, big_ref, sm_ref):
    x = x_ref[...]
    h = x * lax.rsqrt(jnp.mean(x * x, axis=-1, keepdims=True) + EPS) * g_ref[...]
    if n_pad:
        row = pl.program_id(0) * tm + lax.broadcasted_iota(jnp.int32, h.shape, 0)
        h = jnp.where(row >= n_pad, h, 0.0)
    h16 = h.astype(BF16)
    nb = 512
    for j in range(BIG_W // nb):
        big_ref[:, j * nb:(j + 1) * nb] = _mm(h16, wbig_ref[:, j * nb:(j + 1) * nb]).astype(BF16)
    sm_ref[...] = _mm(h16, wsm_ref[...])


def _in_proj(x2d, norm_g, w_big, w_sm, n_pad):
    rows = x2d.shape[0]
    tm = min(ROW_TILE, rows)
    assert rows % tm == 0
    return pl.pallas_call(
        functools.partial(_in_proj_kernel, n_pad, tm),
        grid=(rows // tm,),
        in_specs=[
            pl.BlockSpec((tm, D_MODEL), lambda i: (i, 0)),
            pl.BlockSpec((1, D_MODEL), lambda i: (0, 0)),
            pl.BlockSpec((D_MODEL, BIG_W), lambda i: (0, 0)),
            pl.BlockSpec((D_MODEL, SM_W), lambda i: (0, 0)),
        ],
        out_specs=[
            pl.BlockSpec((tm, BIG_W), lambda i: (i, 0)),
            pl.BlockSpec((tm, SM_W), lambda i: (i, 0)),
        ],
        out_shape=[
            jax.ShapeDtypeStruct((rows, BIG_W), BF16),
            jax.ShapeDtypeStruct((rows, SM_W), F32),
        ],
        compiler_params=pltpu.CompilerParams(
            dimension_semantics=("arbitrary",), vmem_limit_bytes=V7X_VMEM_LIMIT),
        name="in_proj",
    )(x2d, norm_g, w_big, w_sm)


def _stage_gates(st, n_pad, row_base, sm_ref, hist_ref, alog_ref, dtb_ref, w2_ref, gb_ref, lowm_ref):
    C = CHUNK
    rb, rows = st["rb"], st["rows"]
    tril, strict = lowm_ref[0], lowm_ref[1]
    sm = sm_ref[rb, rows, :]
    beta_f = jax.nn.sigmoid(sm)
    g_f = -jnp.exp(alog_ref[...]) * _softplus(sm + dtb_ref[...])
    valid = lambda width: (row_base + st["r0"] + lax.broadcasted_iota(jnp.int32, (C, width), 0)) >= n_pad
    if n_pad:
        g_f = jnp.where(valid(SM_W), g_f, 0.0)
    beta_cols = [beta_f[:, SM_BETA + h:SM_BETA + h + 1] for h in range(DN_HEADS)]
    g_cols = [g_f[:, SM_A + h:SM_A + h + 1] for h in range(DN_HEADS)]
    st["beta64"] = _bcast_heads(beta_cols, C)
    st["beta128"] = _bcast_heads(beta_cols, DN_DK)
    dlog = _cumsum_rows(_bcast_heads(g_cols, C) * strict)
    st["decay"] = jnp.exp(dlog) * tril
    gc_cols = [dlog[:, C * h:C * h + 1] + g_cols[h][0:1, :] for h in range(DN_HEADS)]
    gc128 = _bcast_heads(gc_cols, DN_DK)
    gcl = gc128[C - 1:C, :]
    st["egc"] = jnp.exp(gc128)
    st["ekd"] = jnp.exp(gcl - gc128)
    st["elast"] = jnp.exp(gcl)
    st["xs"] = _mm(st["shift"], hist_ref[rb])
    zg = _mm(sm.astype(BF16), w2_ref[...]) + gb_ref[...]
    gl = -_softplus(-zg) * (1.0 / GLA_NORMALIZER)
    if n_pad:
        gl = jnp.where(valid(GLA_QK), gl, 0.0)
    st["b"] = _cumsum_rows(gl)


def _stage_qk(st, cw, p_ref, m512_ref, m256_ref):
    C = CHUNK
    rb, rows = st["rb"], st["rows"]
    xs = st.pop("xs")
    y = xs[0:C] * cw[0:1, :]
    for i in range(1, CONV_K):
        y = y + xs[i * C:(i + 1) * C] * cw[i:i + 1, :]
    qkv = _silu(y)
    l2 = lambda s: lax.rsqrt(s + EPS)
    q = _headwise(qkv[:, :DN_QK], DN_DK, l2) * (DN_DK ** -0.5)
    k = _headwise(qkv[:, DN_QK:2 * DN_QK], DN_DK, l2)
    v = qkv[:, 2 * DN_QK:]
    kbd = _block_diag(k.astype(BF16), m512_ref)
    st["kkqk"] = _mm_nt(jnp.concatenate([k, q], axis=0).astype(BF16), kbd)
    beta128, egc = st.pop("beta128"), st.pop("egc")
    st["vb16"] = (v * beta128).astype(BF16)
    st["kbg16"] = (k * beta128 * egc).astype(BF16)
    st["qg16"] = (q * egc).astype(BF16)
    st["kd16"] = (k * st.pop("ekd")).astype(BF16)

    b = st.pop("b")
    gq = p_ref[rb, rows, OFF_GQ:OFF_GQ + GLA_QK].astype(F32) * (GLA_DK ** -0.5)
    gk = p_ref[rb, rows, OFF_GK:OFF_GK + GLA_QK].astype(F32)
    bref = b[C // 2:C // 2 + 1, :]
    blast = b[C - 1:C, :]
    qi = gq * jnp.exp(b - bref)
    ki = gk * jnp.exp(bref - b)
    st["a_gla"] = _mm_nt(qi.astype(BF16), _block_diag(ki.astype(BF16), m256_ref))
    st["qg_g16"] = (gq * jnp.exp(b)).astype(BF16)
    st["kd_g16"] = (gk * jnp.exp(blast - b)).astype(BF16)
    st["glast"] = jnp.exp(blast)


def _stage_intra(st, p_ref, m512_ref, lowm_ref):
    C = CHUNK
    tril, strict, eye = lowm_ref[0], lowm_ref[1], lowm_ref[2]
    kkqk, decay = st.pop("kkqk"), st.pop("decay")
    lmat = kkqk[:C] * st.pop("beta64") * decay * strict
    st["attn16"] = (kkqk[C:] * decay).astype(BF16)
    st["pj"] = -lmat
    st["t"] = eye - lmat
    gv16 = p_ref[st["rb"], st["rows"], OFF_GV:OFF_GV + GLA_V]
    st["o_gla_intra"] = _mm((st.pop("a_gla") * tril).astype(BF16), _block_diag(gv16, m512_ref))


def _inverse_step(j, n_sq, st, m256_ref):
    C = CHUNK
    pbd = _block_diag(st["pj"].astype(BF16), m256_ref)
    if j == 0:
        st["pj"] = _mm(st["pj"].astype(BF16), pbd)
    elif j < n_sq:
        both = _mm(jnp.concatenate([st["pj"], st["t"]], axis=0).astype(BF16), pbd)
        st["pj"] = both[:C]
        st["t"] = st["t"] + both[C:]
    else:
        st["t"] = st["t"] + _mm(st["t"].astype(BF16), pbd)


def _stage_uw(st, m512_ref):
    t16 = st.pop("t").astype(BF16)
    st["u"] = _mm(t16, _block_diag(st.pop("vb16"), m512_ref))
    st["w"] = _mm(t16, _block_diag(st.pop("kbg16"), m512_ref))


def _stage_state_read(st, p_ref, mgla_ref, sdn_ref, sgl_ref):
    C = CHUNK
    rb = st["rb"]
    wq = jnp.concatenate([st.pop("w").astype(BF16), st.pop("qg16")], axis=0)
    ws, qs = [], []
    for h in range(DN_HEADS):
        res = _mm(wq[:, DN_DK * h:DN_DK * (h + 1)], sdn_ref[rb, h].astype(BF16))
        ws.append(res[:C])
        qs.append(res[C:])
    st["ws"], st["qs"] = ws, qs
    gv16 = p_ref[rb, st["rows"], OFF_GV:OFF_GV + GLA_V]
    st["o_gla"] = st.pop("o_gla_intra") + _mm_nt(st.pop("qg_g16"), sgl_ref[rb].astype(BF16))
    sgl_ref[rb] = sgl_ref[rb] * st.pop("glast") + _mm_tn(gv16, st.pop("kd_g16")) * mgla_ref[...]


def _stage_state_write(st, m512_ref, sdn_ref):
    rb = st["rb"]
    v_new = st.pop("u") - jnp.concatenate(st.pop("ws"), axis=1)
    vn16 = v_new.astype(BF16)
    st["o_dn"] = jnp.concatenate(st.pop("qs"), axis=1) + _mm(st.pop("attn16"), _block_diag(vn16, m512_ref))
    kd16, elast = st.pop("kd16"), st.pop("elast")
    for h in range(DN_HEADS):
        hs = slice(DN_DV * h, DN_DV * (h + 1))
        sdn_ref[rb, h] = sdn_ref[rb, h] * elast[:, hs] + _mm_tn(kd16[:, hs], vn16[:, hs])


def _stage_out(st, p_ref, dng_ref, glg_ref, mix_ref):
    rb, rows = st["rb"], st["rows"]
    rms = lambda s: lax.rsqrt(s * (1.0 / DN_DV) + EPS)
    z = p_ref[rb, rows, OFF_Z:OFF_Z + DN_V].astype(F32)
    mix_ref[rb, rows, 0:DN_V] = (_headwise(st.pop("o_dn"), DN_DV, rms) * dng_ref[...] * _silu(z)).astype(BF16)
    gr = p_ref[rb, rows, OFF_GR:OFF_GR + GLA_V].astype(F32)
    rms_g = lambda s: lax.rsqrt(s * (1.0 / GLA_DV) + EPS)
    mix_ref[rb, rows, DN_V:MIX_WIDTH] = (
        _headwise(st.pop("o_gla"), GLA_DV, rms_g) * glg_ref[...] * _silu(gr)).astype(BF16)


def _mixer_kernel(n_pad, NB, R, p_ref, sm_ref, cw_ref, alog_ref, dtb_ref, dng_ref, glg_ref, w2_ref, gb_ref,
                  m512_ref, m256_ref, mgla_ref, lowm_ref, shift_ref, sdn0_ref, sgl0_ref, tail0_ref,
                  mix_ref, sdn_ref, sgl_ref, tail_ref, hist_ref):
    i = pl.program_id(1)

    @pl.when(i == 0)
    def _():
        for rb in range(NB):
            hist_ref[rb, CHUNK:2 * CHUNK, :] = tail0_ref[0]
            sdn_ref[rb] = sdn0_ref[0]
            sgl_ref[rb] = sgl0_ref[0]

    parity = i % 2
    qkv_rows = p_ref[:, :, OFF_QKV:OFF_QKV + CONV_W]
    hist_ref[:, pl.ds(pl.multiple_of(parity * CHUNK, CHUNK), CHUNK), :] = qkv_rows
    tail_ref[...] = qkv_rows
    shift = shift_ref[parity]
    cw = cw_ref[...]
    n_chunks = R // CHUNK
    sts = [dict(rb=rb, r0=n * CHUNK, rows=slice(n * CHUNK, (n + 1) * CHUNK), n=n, shift=shift)
           for n in range(n_chunks) for rb in range(NB)]
    for st in sts:
        _stage_gates(st, n_pad, i * R, sm_ref, hist_ref, alog_ref, dtb_ref, w2_ref, gb_ref, lowm_ref)
    for st in sts:
        _stage_qk(st, cw, p_ref, m512_ref, m256_ref)
    for st in sts:
        _stage_intra(st, p_ref, m512_ref, lowm_ref)
    n_sq = CHUNK.bit_length() - 2
    for j in range(n_sq + 1):
        for st in sts:
            _inverse_step(j, n_sq, st, m256_ref)
    for st in sts:
        _stage_uw(st, m512_ref)
    for n in range(n_chunks):
        group = [st for st in sts if st["n"] == n]
        for st in group:
            _stage_state_read(st, p_ref, mgla_ref, sdn_ref, sgl_ref)
        for st in group:
            _stage_state_write(st, m512_ref, sdn_ref)
        for st in group:
            _stage_out(st, p_ref, dng_ref, glg_ref, mix_ref)


def _mixer_masks():
    r = jnp.arange(4 * CHUNK)[:, None]
    m512 = (r // CHUNK == jnp.arange(DN_V)[None, :] // DN_DV).astype(BF16)
    m256 = (r // CHUNK == jnp.arange(4 * CHUNK)[None, :] // CHUNK).astype(BF16)
    mgla = (jnp.arange(GLA_V)[:, None] // GLA_DV == jnp.arange(GLA_QK)[None, :] // GLA_DK).astype(F32)
    c = jnp.arange(CHUNK)[:, None]
    s = jnp.arange(4 * CHUNK)[None, :] % CHUNK
    lowm = jnp.stack([c >= s, c > s, c == s]).astype(F32)
    tap = jnp.arange(CONV_K * CHUNK)[:, None] // CHUNK
    pos = jnp.arange(CONV_K * CHUNK)[:, None] % CHUNK - (CONV_K - 1) + tap
    col = jnp.arange(2 * CHUNK)[None, :]
    shift = jnp.stack([col == jnp.where(pos >= 0, pos, pos + 2 * CHUNK),
                       col == pos + CHUNK]).astype(BF16)
    return m512, m256, mgla, lowm, shift


def _mixer(big, sm, consts, sdn0, sgl0, tail0, n_pad):
    B, T, _ = big.shape
    NB = min(MIXER_BATCH, B)
    R = CHUNK
    assert B % NB == 0 and T % R == 0 and R % CHUNK == 0
    masks = _mixer_masks()
    const_arrays = list(consts) + list(masks)
    full = lambda a: pl.BlockSpec(a.shape, lambda b, i: (0,) * a.ndim)
    return pl.pallas_call(
        functools.partial(_mixer_kernel, n_pad, NB, R),
        grid=(B // NB, T // R),
        in_specs=[
            pl.BlockSpec((NB, R, BIG_W), lambda b, i: (b, i, 0)),
            pl.BlockSpec((NB, R, SM_W), lambda b, i: (b, i, 0)),
        ] + [full(a) for a in const_arrays] + [full(sdn0), full(sgl0), full(tail0)],
        out_specs=[
            pl.BlockSpec((NB, R, MIX_WIDTH), lambda b, i: (b, i, 0)),
            pl.BlockSpec((NB, DN_HEADS, DN_DK, DN_DV), lambda b, i: (b, 0, 0, 0)),
            pl.BlockSpec((NB, GLA_V, GLA_QK), lambda b, i: (b, 0, 0)),
            pl.BlockSpec((NB, CHUNK, CONV_W), lambda b, i: (b, 0, 0)),
        ],
        out_shape=[
            jax.ShapeDtypeStruct((B, T, MIX_WIDTH), BF16),
            jax.ShapeDtypeStruct((B, DN_HEADS, DN_DK, DN_DV), F32),
            jax.ShapeDtypeStruct((B, GLA_V, GLA_QK), F32),
            jax.ShapeDtypeStruct((B, CHUNK, CONV_W), BF16),
        ],
        scratch_shapes=[pltpu.VMEM((NB, 2 * CHUNK, CONV_W), BF16)],
        compiler_params=pltpu.CompilerParams(
            dimension_semantics=("arbitrary", "arbitrary"), vmem_limit_bytes=V7X_VMEM_LIMIT),
        name="mixer",
    )(big, sm, *const_arrays, sdn0, sgl0, tail0)


def _out_mlp_kernel(x_ref, mix_ref, wout_ref, g2_ref, wup_ref, wdown_ref, gf_ref, o_ref):
    x1 = x_ref[...] + _mm(mix_ref[...], wout_ref[...])
    h2 = (x1 * lax.rsqrt(jnp.mean(x1 * x1, axis=-1, keepdims=True) + EPS) * g2_ref[...]).astype(BF16)
    acc = x1
    nb = 1024
    for j in range(D_FF // nb):
        a = jnp.maximum(_mm(h2, wup_ref[:, j * nb:(j + 1) * nb]), 0.0)
        acc = acc + _mm((a * a).astype(BF16), wdown_ref[j * nb:(j + 1) * nb, :])
    o_ref[...] = acc * lax.rsqrt(jnp.mean(acc * acc, axis=-1, keepdims=True) + EPS) * gf_ref[...]


def _out_mlp(x2d, mix2d, w_out, norm2_g, w_up, w_down, final_g):
    rows = x2d.shape[0]
    tm = min(ROW_TILE, rows)
    assert rows % tm == 0
    const = lambda a: pl.BlockSpec(a.shape, lambda i: (0, 0), pipeline_mode=pl.Buffered(1))
    return pl.pallas_call(
        _out_mlp_kernel,
        grid=(rows // tm,),
        in_specs=[
            pl.BlockSpec((tm, D_MODEL), lambda i: (i, 0)),
            pl.BlockSpec((tm, MIX_WIDTH), lambda i: (i, 0)),
            const(w_out), const(norm2_g), const(w_up), const(w_down), const(final_g),
        ],
        out_specs=pl.BlockSpec((tm, D_MODEL), lambda i: (i, 0)),
        out_shape=jax.ShapeDtypeStruct((rows, D_MODEL), F32),
        compiler_params=pltpu.CompilerParams(
            dimension_semantics=("arbitrary",), vmem_limit_bytes=V7X_VMEM_LIMIT),
        name="out_mlp",
    )(x2d, mix2d, w_out, norm2_g, w_up, w_down, final_g)


def _pad_lanes(row, offset, width):
    return jnp.zeros((1, width), F32).at[0, offset:offset + row.shape[0]].set(row.astype(F32))


def kernel(x, meta_tokens, norm1_g, w_in, conv_w, a_log, dt_bias, dn_norm_g, gla_w2, gla_b, gla_norm_g,
           w_out, norm2_g, w_up, w_down, final_norm_g):
    B, S, D = x.shape
    assert norm1_g.shape[0] == 1 and D == D_MODEL and S % CHUNK == 0

    offs = [0]
    for width in (DN_QK, DN_QK, DN_V, DN_V, DN_HEADS, DN_HEADS, GLA_QK, GLA_QK, GLA_V, GLA_V, GLA_RANK):
        offs.append(offs[-1] + width)
    w = w_in[0]
    col = lambda i: w[:, offs[i]:offs[i + 1]]
    w_big = jnp.concatenate([col(0), col(1), col(2), col(3), col(6), col(7), col(8), col(9)], axis=1).astype(BF16)
    w_sm = jnp.concatenate(
        [col(4), col(5), col(10), jnp.zeros((D, SM_W - 2 * DN_HEADS - GLA_RANK), w.dtype)], axis=1).astype(BF16)

    w2_pad = jnp.zeros((SM_W, GLA_QK), F32).at[SM_GLR:SM_GLR + GLA_RANK].set(gla_w2[0]).astype(BF16)
    consts = (
        conv_w[0].astype(F32),
        _pad_lanes(a_log[0], SM_A, SM_W),
        _pad_lanes(dt_bias[0], SM_A, SM_W),
        jnp.tile(dn_norm_g[0].astype(F32), DN_HEADS)[None, :],
        jnp.tile(gla_norm_g[0].astype(F32), GLA_HEADS)[None, :],
        w2_pad,
        gla_b[0].astype(F32)[None, :],
    )
    g1 = norm1_g[0].astype(F32)[None, :]

    prefix = jnp.concatenate([jnp.zeros((N_PAD, D), x.dtype), meta_tokens.astype(x.dtype)], axis=0)
    big0, sm0 = _in_proj(prefix, g1, w_big, w_sm, N_PAD)
    zeros = lambda *shape: jnp.zeros(shape, F32)
    _, sdn0, sgl0, tail0 = _mixer(
        big0[None], sm0[None], consts,
        zeros(1, DN_HEADS, DN_DK, DN_DV), zeros(1, GLA_V, GLA_QK), jnp.zeros((1, CHUNK, CONV_W), BF16), N_PAD)

    x2d = x.reshape(B * S, D)
    big, sm = _in_proj(x2d, g1, w_big, w_sm, 0)
    mix, _, _, _ = _mixer(big.reshape(B, S, BIG_W), sm.reshape(B, S, SM_W), consts, sdn0, sgl0, tail0, 0)
    out = _out_mlp(x2d, mix.reshape(B * S, MIX_WIDTH), w_out[0].astype(BF16), norm2_g[0].astype(F32)[None, :],
                   w_up[0].astype(BF16), w_down[0].astype(BF16), final_norm_g.astype(F32)[None, :])
    return out.reshape(B, S, D)
```

```python
import functools

import jax
import jax.numpy as jnp
from jax import lax
from jax.experimental import pallas as pl
from jax.experimental.pallas import tpu as pltpu

F32 = jnp.float32
BF16 = jnp.bfloat16

D_MODEL = 1024
N_META = 16
CHUNK = 64
N_PAD = CHUNK - N_META
DN_HEADS = 4
DN_DK = 128
DN_DV = 128
CONV_K = 4
GLA_HEADS = 4
GLA_DK = 64
GLA_DV = 128
GLA_RANK = 16
GLA_NORMALIZER = 16.0
D_FF = 4 * D_MODEL
EPS = 1e-6

DN_QK = DN_HEADS * DN_DK
DN_V = DN_HEADS * DN_DV
GLA_QK = GLA_HEADS * GLA_DK
GLA_V = GLA_HEADS * GLA_DV
MIX_WIDTH = DN_V + GLA_V
CONV_W = 2 * DN_QK + DN_V

OFF_QKV = 0
OFF_Z = OFF_QKV + CONV_W
OFF_GQ = OFF_Z + DN_V
OFF_GK = OFF_GQ + GLA_QK
OFF_GV = OFF_GK + GLA_QK
OFF_GR = OFF_GV + GLA_V
BIG_W = OFF_GR + GLA_V
SM_W = 128
SM_BETA = 0
SM_A = DN_HEADS
SM_GLR = 2 * DN_HEADS

ROW_TILE = 512
MIXER_BATCH = 4
V7X_VMEM_LIMIT = 56 * 1024 * 1024


def _mm(a, b):
    return jnp.dot(a, b, preferred_element_type=F32)


def _mm_nt(a, b):
    return lax.dot_general(a, b, (((1,), (1,)), ((), ())), preferred_element_type=F32)


def _mm_tn(a, b):
    return lax.dot_general(a, b, (((0,), (0,)), ((), ())), preferred_element_type=F32)


def _softplus(x):
    return jnp.maximum(x, 0.0) + jnp.log1p(jnp.exp(-jnp.abs(x)))


def _silu(x):
    return x * jax.nn.sigmoid(x)


def _cumsum_rows(x):
    row = lax.broadcasted_iota(jnp.int32, x.shape, 0)
    s = 1
    while s < x.shape[0]:
        x = x + jnp.where(row >= s, pltpu.roll(x, s, 0), 0.0)
        s *= 2
    return x


def _headwise(x, width, fn):
    outs = []
    for h in range(x.shape[1] // width):
        xs = x[:, h * width:(h + 1) * width]
        outs.append(xs * fn(jnp.sum(xs * xs, axis=-1, keepdims=True)))
    return jnp.concatenate(outs, axis=1)


def _bcast_heads(cols, width):
    c, n = cols[0].shape[0], len(cols)
    if width % 128 == 0:
        return jnp.concatenate([jnp.broadcast_to(col, (c, width)) for col in cols], axis=1)
    lane = lax.broadcasted_iota(jnp.int32, (c, n * width), 1)
    out = jnp.broadcast_to(cols[-1], (c, n * width))
    for h in reversed(range(n - 1)):
        out = jnp.where(lane < (h + 1) * width, jnp.broadcast_to(cols[h], (c, n * width)), out)
    return out


def _block_diag(x16, mask_ref):
    return jnp.concatenate([x16] * 4, axis=0) * mask_ref[...]


def _in_proj_kernel(n_pad, tm, x_ref, g_ref, wbig_ref, wsm_ref, big_ref, sm_ref):
    x = x_ref[...]
    h = x * lax.rsqrt(jnp.mean(x * x, axis=-1, keepdims=True) + EPS) * g_ref[...]
    if n_pad:
        row = pl.program_id(0) * tm + lax.broadcasted_iota(jnp.int32, h.shape, 0)
        h = jnp.where(row >= n_pad, h, 0.0)
    h16 = h.astype(BF16)
    nb = 512
    for j in range(BIG_W // nb):
        big_ref[:, j * nb:(j + 1) * nb] = _mm(h16, wbig_ref[:, j * nb:(j + 1) * nb]).astype(BF16)
    sm_ref[...] = _mm(h16, wsm_ref[...])


def _in_proj(x2d, norm_g, w_big, w_sm, n_pad):
    rows = x2d.shape[0]
    tm = min(ROW_TILE, rows)
    assert rows % tm == 0
    return pl.pallas_call(
        functools.partial(_in_proj_kernel, n_pad, tm),
        grid=(rows // tm,),
        in_specs=[
            pl.BlockSpec((tm, D_MODEL), lambda i: (i, 0)),
            pl.BlockSpec((1, D_MODEL), lambda i: (0, 0)),
            pl.BlockSpec((D_MODEL, BIG_W), lambda i: (0, 0)),
            pl.BlockSpec((D_MODEL, SM_W), lambda i: (0, 0)),
        ],
        out_specs=[
            pl.BlockSpec((tm, BIG_W), lambda i: (i, 0)),
            pl.BlockSpec((tm, SM_W), lambda i: (i, 0)),
        ],
        out_shape=[
            jax.ShapeDtypeStruct((rows, BIG_W), BF16),
            jax.ShapeDtypeStruct((rows, SM_W), F32),
        ],
        compiler_params=pltpu.CompilerParams(
            dimension_semantics=("arbitrary",), vmem_limit_bytes=V7X_VMEM_LIMIT),
        name="in_proj",
    )(x2d, norm_g, w_big, w_sm)


def _stage_gates(st, n_pad, row_base, sm_ref, hist_ref, alog_ref, dtb_ref, w2_ref, gb_ref, lowm_ref):
    C = CHUNK
    rb, rows = st["rb"], st["rows"]
    tril, strict = lowm_ref[0], lowm_ref[1]
    sm = sm_ref[rb, rows, :]
    beta_f = jax.nn.sigmoid(sm)
    g_f = -jnp.exp(alog_ref[...]) * _softplus(sm + dtb_ref[...])
    valid = lambda width: (row_base + st["r0"] + lax.broadcasted_iota(jnp.int32, (C, width), 0)) >= n_pad
    if n_pad:
        g_f = jnp.where(valid(SM_W), g_f, 0.0)
    beta_cols = [beta_f[:, SM_BETA + h:SM_BETA + h + 1] for h in range(DN_HEADS)]
    g_cols = [g_f[:, SM_A + h:SM_A + h + 1] for h in range(DN_HEADS)]
    st["beta64"] = _bcast_heads(beta_cols, C)
    st["beta128"] = _bcast_heads(beta_cols, DN_DK)
    dlog = _cumsum_rows(_bcast_heads(g_cols, C) * strict)
    st["decay"] = jnp.exp(dlog) * tril
    gc_cols = [dlog[:, C * h:C * h + 1] + g_cols[h][0:1, :] for h in range(DN_HEADS)]
    gc128 = _bcast_heads(gc_cols, DN_DK)
    gcl = gc128[C - 1:C, :]
    st["egc"] = jnp.exp(gc128)
    st["ekd"] = jnp.exp(gcl - gc128)
    st["elast"] = jnp.exp(gcl)
    st["xs"] = _mm(st["shift"], hist_ref[rb])
    zg = _mm(sm.astype(BF16), w2_ref[...]) + gb_ref[...]
    gl = -_softplus(-zg) * (1.0 / GLA_NORMALIZER)
    if n_pad:
        gl = jnp.where(valid(GLA_QK), gl, 0.0)
    st["b"] = _cumsum_rows(gl)


def _stage_qk(st, cw, p_ref, m512_ref, m256_ref):
    C = CHUNK
    rb, rows = st["rb"], st["rows"]
    xs = st.pop("xs")
    y = xs[0:C] * cw[0:1, :]
    for i in range(1, CONV_K):
        y = y + xs[i * C:(i + 1) * C] * cw[i:i + 1, :]
    qkv = _silu(y)
    l2 = lambda s: lax.rsqrt(s + EPS)
    q = _headwise(qkv[:, :DN_QK], DN_DK, l2) * (DN_DK ** -0.5)
    k = _headwise(qkv[:, DN_QK:2 * DN_QK], DN_DK, l2)
    v = qkv[:, 2 * DN_QK:]
    kbd = _block_diag(k.astype(BF16), m512_ref)
    st["kkqk"] = _mm_nt(jnp.concatenate([k, q], axis=0).astype(BF16), kbd)
    beta128, egc = st.pop("beta128"), st.pop("egc")
    st["vb16"] = (v * beta128).astype(BF16)
    st["kbg16"] = (k * beta128 * egc).astype(BF16)
    st["qg16"] = (q * egc).astype(BF16)
    st["kd16"] = (k * st.pop("ekd")).astype(BF16)

    b = st.pop("b")
    gq = p_ref[rb, rows, OFF_GQ:OFF_GQ + GLA_QK].astype(F32) * (GLA_DK ** -0.5)
    gk = p_ref[rb, rows, OFF_GK:OFF_GK + GLA_QK].astype(F32)
    bref = b[C // 2:C // 2 + 1, :]
    blast = b[C - 1:C, :]
    qi = gq * jnp.exp(b - bref)
    ki = gk * jnp.exp(bref - b)
    st["a_gla"] = _mm_nt(qi.astype(BF16), _block_diag(ki.astype(BF16), m256_ref))
    st["qg_g16"] = (gq * jnp.exp(b)).astype(BF16)
    st["kd_g16"] = (gk * jnp.exp(blast - b)).astype(BF16)
    st["glast"] = jnp.exp(blast)


def _stage_intra(st, p_ref, m512_ref, lowm_ref):
    C = CHUNK
    tril, strict, eye = lowm_ref[0], lowm_ref[1], lowm_ref[2]
    kkqk, decay = st.pop("kkqk"), st.pop("decay")
    lmat = kkqk[:C] * st.pop("beta64") * decay * strict
    st["attn16"] = (kkqk[C:] * decay).astype(BF16)
    st["pj"] = -lmat
    st["t"] = eye - lmat
    gv16 = p_ref[st["rb"], st["rows"], OFF_GV:OFF_GV + GLA_V]
    st["o_gla_intra"] = _mm((st.pop("a_gla") * tril).astype(BF16), _block_diag(gv16, m512_ref))


def _inverse_step(j, n_sq, st, m256_ref):
    C = CHUNK
    pbd = _block_diag(st["pj"].astype(BF16), m256_ref)
    if j == 0:
        st["pj"] = _mm(st["pj"].astype(BF16), pbd)
    elif j < n_sq:
        both = _mm(jnp.concatenate([st["pj"], st["t"]], axis=0).astype(BF16), pbd)
        st["pj"] = both[:C]
        st["t"] = st["t"] + both[C:]
    else:
        st["t"] = st["t"] + _mm(st["t"].astype(BF16), pbd)


def _stage_uw(st, m512_ref):
    t16 = st.pop("t").astype(BF16)
    st["u"] = _mm(t16, _block_diag(st.pop("vb16"), m512_ref))
    st["w"] = _mm(t16, _block_diag(st.pop("kbg16"), m512_ref))


def _stage_state_read(st, p_ref, mgla_ref, sdn_ref, sgl_ref):
    C = CHUNK
    rb = st["rb"]
    wq = jnp.concatenate([st.pop("w").astype(BF16), st.pop("qg16")], axis=0)
    ws, qs = [], []
    for h in range(DN_HEADS):
        res = _mm(wq[:, DN_DK * h:DN_DK * (h + 1)], sdn_ref[rb, h].astype(BF16))
        ws.append(res[:C])
        qs.append(res[C:])
    st["ws"], st["qs"] = ws, qs
    gv16 = p_ref[rb, st["rows"], OFF_GV:OFF_GV + GLA_V]
    st["o_gla"] = st.pop("o_gla_intra") + _mm_nt(st.pop("qg_g16"), sgl_ref[rb].astype(BF16))
    sgl_ref[rb] = sgl_ref[rb] * st.pop("glast") + _mm_tn(gv16, st.pop("kd_g16")) * mgla_ref[...]


def _stage_state_write(st, m512_ref, sdn_ref):
    rb = st["rb"]
    v_new = st.pop("u") - jnp.concatenate(st.pop("ws"), axis=1)
    vn16 = v_new.astype(BF16)
    st["o_dn"] = jnp.concatenate(st.pop("qs"), axis=1) + _mm(st.pop("attn16"), _block_diag(vn16, m512_ref))
    kd16, elast = st.pop("kd16"), st.pop("elast")
    for h in range(DN_HEADS):
        hs = slice(DN_DV * h, DN_DV * (h + 1))
        sdn_ref[rb, h] = sdn_ref[rb, h] * elast[:, hs] + _mm_tn(kd16[:, hs], vn16[:, hs])


def _stage_out(st, p_ref, dng_ref, glg_ref, mix_ref):
    rb, rows = st["rb"], st["rows"]
    rms = lambda s: lax.rsqrt(s * (1.0 / DN_DV) + EPS)
    z = p_ref[rb, rows, OFF_Z:OFF_Z + DN_V].astype(F32)
    mix_ref[rb, rows, 0:DN_V] = (_headwise(st.pop("o_dn"), DN_DV, rms) * dng_ref[...] * _silu(z)).astype(BF16)
    gr = p_ref[rb, rows, OFF_GR:OFF_GR + GLA_V].astype(F32)
    rms_g = lambda s: lax.rsqrt(s * (1.0 / GLA_DV) + EPS)
    mix_ref[rb, rows, DN_V:MIX_WIDTH] = (
        _headwise(st.pop("o_gla"), GLA_DV, rms_g) * glg_ref[...] * _silu(gr)).astype(BF16)


def _mixer_step(i, n_pad, NB, p_ref, sm_ref, consts, sdn0_ref, sgl0_ref, tail0_ref,
                mix_ref, sdn_ref, sgl_ref, hist_ref, fillers=()):
    cw_ref, alog_ref, dtb_ref, dng_ref, glg_ref, w2_ref, gb_ref, m512_ref, m256_ref, mgla_ref, lowm_ref, shift_ref = consts
    fill = iter(fillers)

    def gap():
        piece = next(fill, None)
        if piece is not None:
            piece()

    @pl.when(i == 0)
    def _():
        for rb in range(NB):
            hist_ref[rb, CHUNK:2 * CHUNK, :] = tail0_ref[0]
            sdn_ref[rb] = sdn0_ref[0]
            sgl_ref[rb] = sgl0_ref[0]

    parity = i % 2
    hist_ref[:, pl.ds(pl.multiple_of(parity * CHUNK, CHUNK), CHUNK), :] = p_ref[:, :, OFF_QKV:OFF_QKV + CONV_W]
    shift = shift_ref[parity]
    cw = cw_ref[...]
    sts = [dict(rb=rb, r0=0, rows=slice(0, CHUNK), shift=shift) for rb in range(NB)]
    gap()
    for st in sts:
        _stage_gates(st, n_pad, i * CHUNK, sm_ref, hist_ref, alog_ref, dtb_ref, w2_ref, gb_ref, lowm_ref)
    gap()
    for st in sts:
        _stage_qk(st, cw, p_ref, m512_ref, m256_ref)
    gap()
    for st in sts:
        _stage_intra(st, p_ref, m512_ref, lowm_ref)
    n_sq = CHUNK.bit_length() - 2
    for j in range(n_sq + 1):
        gap()
        for st in sts:
            _inverse_step(j, n_sq, st, m256_ref)
    gap()
    for st in sts:
        _stage_uw(st, m512_ref)
    gap()
    for st in sts:
        _stage_state_read(st, p_ref, mgla_ref, sdn_ref, sgl_ref)
    gap()
    for st in sts:
        _stage_state_write(st, m512_ref, sdn_ref)
    for piece in fill:
        piece()
    for st in sts:
        _stage_out(st, p_ref, dng_ref, glg_ref, mix_ref)


N_MIXER_CONSTS = 12


def _mixer_kernel(n_pad, NB, *refs):
    p_ref, sm_ref = refs[:2]
    consts = refs[2:2 + N_MIXER_CONSTS]
    sdn0_ref, sgl0_ref, tail0_ref, mix_ref, sdn_ref, sgl_ref, tail_ref, hist_ref = refs[2 + N_MIXER_CONSTS:]
    tail_ref[...] = p_ref[:, :, OFF_QKV:OFF_QKV + CONV_W]
    _mixer_step(pl.program_id(1), n_pad, NB, p_ref, sm_ref, consts, sdn0_ref, sgl0_ref, tail0_ref,
                mix_ref, sdn_ref, sgl_ref, hist_ref)


def _mixer_masks():
    r = jnp.arange(4 * CHUNK)[:, None]
    m512 = (r // CHUNK == jnp.arange(DN_V)[None, :] // DN_DV).astype(BF16)
    m256 = (r // CHUNK == jnp.arange(4 * CHUNK)[None, :] // CHUNK).astype(BF16)
    mgla = (jnp.arange(GLA_V)[:, None] // GLA_DV == jnp.arange(GLA_QK)[None, :] // GLA_DK).astype(F32)
    c = jnp.arange(CHUNK)[:, None]
    s = jnp.arange(4 * CHUNK)[None, :] % CHUNK
    lowm = jnp.stack([c >= s, c > s, c == s]).astype(F32)
    tap = jnp.arange(CONV_K * CHUNK)[:, None] // CHUNK
    pos = jnp.arange(CONV_K * CHUNK)[:, None] % CHUNK - (CONV_K - 1) + tap
    col = jnp.arange(2 * CHUNK)[None, :]
    shift = jnp.stack([col == jnp.where(pos >= 0, pos, pos + 2 * CHUNK),
                       col == pos + CHUNK]).astype(BF16)
    return m512, m256, mgla, lowm, shift


def _mixer(big, sm, consts, sdn0, sgl0, tail0, n_pad):
    B, T, _ = big.shape
    NB = min(MIXER_BATCH, B)
    R = CHUNK
    assert B % NB == 0 and T % R == 0 and R % CHUNK == 0
    masks = _mixer_masks()
    const_arrays = list(consts) + list(masks)
    full = lambda a: pl.BlockSpec(a.shape, lambda b, i: (0,) * a.ndim)
    return pl.pallas_call(
        functools.partial(_mixer_kernel, n_pad, NB),
        grid=(B // NB, T // R),
        in_specs=[
            pl.BlockSpec((NB, R, BIG_W), lambda b, i: (b, i, 0)),
            pl.BlockSpec((NB, R, SM_W), lambda b, i: (b, i, 0)),
        ] + [full(a) for a in const_arrays] + [full(sdn0), full(sgl0), full(tail0)],
        out_specs=[
            pl.BlockSpec((NB, R, MIX_WIDTH), lambda b, i: (b, i, 0)),
            pl.BlockSpec((NB, DN_HEADS, DN_DK, DN_DV), lambda b, i: (b, 0, 0, 0)),
            pl.BlockSpec((NB, GLA_V, GLA_QK), lambda b, i: (b, 0, 0)),
            pl.BlockSpec((NB, CHUNK, CONV_W), lambda b, i: (b, 0, 0)),
        ],
        out_shape=[
            jax.ShapeDtypeStruct((B, T, MIX_WIDTH), BF16),
            jax.ShapeDtypeStruct((B, DN_HEADS, DN_DK, DN_DV), F32),
            jax.ShapeDtypeStruct((B, GLA_V, GLA_QK), F32),
            jax.ShapeDtypeStruct((B, CHUNK, CONV_W), BF16),
        ],
        scratch_shapes=[pltpu.VMEM((NB, 2 * CHUNK, CONV_W), BF16)],
        compiler_params=pltpu.CompilerParams(
            dimension_semantics=("arbitrary", "arbitrary"), vmem_limit_bytes=V7X_VMEM_LIMIT),
        name="mixer",
    )(big, sm, *const_arrays, sdn0, sgl0, tail0)


def _mlp_pieces(NB, x_ref, mixp_ref, wout_ref, g2_ref, wup_ref, wdown_ref, gf_ref, o_ref):
    env = {}
    nb = 1024

    def p_out():
        x = jnp.concatenate([x_ref[rb] for rb in range(NB)], axis=0)
        mix = jnp.concatenate([mixp_ref[rb] for rb in range(NB)], axis=0)
        x1 = x + _mm(mix, wout_ref[...])
        env["h2"] = (x1 * lax.rsqrt(jnp.mean(x1 * x1, axis=-1, keepdims=True) + EPS) * g2_ref[...]).astype(BF16)
        env["acc"] = x1

    def p_up(j):
        a = jnp.maximum(_mm(env["h2"], wup_ref[:, j * nb:(j + 1) * nb]), 0.0)
        env["a"] = (a * a).astype(BF16)

    def p_down(j):
        env["acc"] = env["acc"] + _mm(env.pop("a"), wdown_ref[j * nb:(j + 1) * nb, :])

    def p_final():
        acc = env.pop("acc")
        out = acc * lax.rsqrt(jnp.mean(acc * acc, axis=-1, keepdims=True) + EPS) * gf_ref[...]
        for rb in range(NB):
            o_ref[rb] = out[rb * CHUNK:(rb + 1) * CHUNK]

    pieces = [p_out]
    for j in range(D_FF // nb):
        pieces += [functools.partial(p_up, j), functools.partial(p_down, j)]
    return pieces + [p_final]


def _mixer_mlp_kernel(NB, NT, *refs):
    p_ref, sm_ref, x_ref = refs[:3]
    consts = refs[3:3 + N_MIXER_CONSTS]
    (sdn0_ref, sgl0_ref, tail0_ref, wout_ref, g2_ref, wup_ref, wdown_ref, gf_ref,
     o_ref, mix_ref, sdn_ref, sgl_ref, hist_ref) = refs[3 + N_MIXER_CONSTS:]
    s = pl.program_id(0)
    i = s % NT
    half = i % 2

    @pl.when(s == 0)
    def _():
        mix_ref[1] = jnp.zeros(mix_ref.shape[1:], mix_ref.dtype)

    pieces = _mlp_pieces(NB, x_ref, mix_ref.at[1 - half], wout_ref, g2_ref, wup_ref, wdown_ref, gf_ref, o_ref)
    _mixer_step(i, 0, NB, p_ref, sm_ref, consts, sdn0_ref, sgl0_ref, tail0_ref,
                mix_ref.at[half], sdn_ref, sgl_ref, hist_ref, fillers=pieces)


def _mixer_mlp(big, sm, x, consts, sdn0, sgl0, tail0, w_out, norm2_g, w_up, w_down, final_g):
    B, T, _ = big.shape
    NB = min(MIXER_BATCH, B)
    NT = T // CHUNK
    assert B % NB == 0 and T % CHUNK == 0 and NT % 2 == 0
    last = (B // NB) * NT - 1
    const_arrays = list(consts) + list(_mixer_masks())
    assert len(const_arrays) == N_MIXER_CONSTS
    once = lambda a: pl.BlockSpec(a.shape, lambda s: (0,) * a.ndim, pipeline_mode=pl.Buffered(1))
    cur = lambda s: (jnp.minimum(s, last) // NT, jnp.minimum(s, last) % NT, 0)
    prev = lambda s: (jnp.maximum(s - 1, 0) // NT, jnp.maximum(s - 1, 0) % NT, 0)
    weights = [w_out, norm2_g, w_up, w_down, final_g]
    return pl.pallas_call(
        functools.partial(_mixer_mlp_kernel, NB, NT),
        grid=(last + 2,),
        in_specs=[
            pl.BlockSpec((NB, CHUNK, BIG_W), cur),
            pl.BlockSpec((NB, CHUNK, SM_W), cur),
            pl.BlockSpec((NB, CHUNK, D_MODEL), prev),
        ] + [once(a) for a in const_arrays + [sdn0, sgl0, tail0] + weights],
        out_specs=pl.BlockSpec((NB, CHUNK, D_MODEL), prev),
        out_shape=jax.ShapeDtypeStruct((B, T, D_MODEL), F32),
        scratch_shapes=[
            pltpu.VMEM((2, NB, CHUNK, MIX_WIDTH), BF16),
            pltpu.VMEM((NB, DN_HEADS, DN_DK, DN_DV), F32),
            pltpu.VMEM((NB, GLA_V, GLA_QK), F32),
            pltpu.VMEM((NB, 2 * CHUNK, CONV_W), BF16),
        ],
        compiler_params=pltpu.CompilerParams(
            dimension_semantics=("arbitrary",), vmem_limit_bytes=V7X_VMEM_LIMIT),
        name="mixer_mlp",
    )(big, sm, x, *const_arrays, sdn0, sgl0, tail0, *weights)


def _pad_lanes(row, offset, width):
    return jnp.zeros((1, width), F32).at[0, offset:offset + row.shape[0]].set(row.astype(F32))


def kernel(x, meta_tokens, norm1_g, w_in, conv_w, a_log, dt_bias, dn_norm_g, gla_w2, gla_b, gla_norm_g,
           w_out, norm2_g, w_up, w_down, final_norm_g):
    B, S, D = x.shape
    assert norm1_g.shape[0] == 1 and D == D_MODEL and S % CHUNK == 0

    offs = [0]
    for width in (DN_QK, DN_QK, DN_V, DN_V, DN_HEADS, DN_HEADS, GLA_QK, GLA_QK, GLA_V, GLA_V, GLA_RANK):
        offs.append(offs[-1] + width)
    w = w_in[0]
    col = lambda i: w[:, offs[i]:offs[i + 1]]
    w_big = jnp.concatenate([col(0), col(1), col(2), col(3), col(6), col(7), col(8), col(9)], axis=1).astype(BF16)
    w_sm = jnp.concatenate(
        [col(4), col(5), col(10), jnp.zeros((D, SM_W - 2 * DN_HEADS - GLA_RANK), w.dtype)], axis=1).astype(BF16)

    w2_pad = jnp.zeros((SM_W, GLA_QK), F32).at[SM_GLR:SM_GLR + GLA_RANK].set(gla_w2[0]).astype(BF16)
    consts = (
        conv_w[0].astype(F32),
        _pad_lanes(a_log[0], SM_A, SM_W),
        _pad_lanes(dt_bias[0], SM_A, SM_W),
        jnp.tile(dn_norm_g[0].astype(F32), DN_HEADS)[None, :],
        jnp.tile(gla_norm_g[0].astype(F32), GLA_HEADS)[None, :],
        w2_pad,
        gla_b[0].astype(F32)[None, :],
    )
    g1 = norm1_g[0].astype(F32)[None, :]

    prefix = jnp.concatenate([jnp.zeros((N_PAD, D), x.dtype), meta_tokens.astype(x.dtype)], axis=0)
    big0, sm0 = _in_proj(prefix, g1, w_big, w_sm, N_PAD)
    zeros = lambda *shape: jnp.zeros(shape, F32)
    _, sdn0, sgl0, tail0 = _mixer(
        big0[None], sm0[None], consts,
        zeros(1, DN_HEADS, DN_DK, DN_DV), zeros(1, GLA_V, GLA_QK), jnp.zeros((1, CHUNK, CONV_W), BF16), N_PAD)

    big, sm = _in_proj(x.reshape(B * S, D), g1, w_big, w_sm, 0)
    return _mixer_mlp(big.reshape(B, S, BIG_W), sm.reshape(B, S, SM_W), x, consts, sdn0, sgl0, tail0,
                      w_out[0].astype(BF16), norm2_g[0].astype(F32)[None, :], w_up[0].astype(BF16),
                      w_down[0].astype(BF16), final_norm_g.astype(F32)[None, :])
```

```python
import functools

import jax
import jax.numpy as jnp
from jax import lax
from jax.experimental import pallas as pl
from jax.experimental.pallas import tpu as pltpu

F32 = jnp.float32
BF16 = jnp.bfloat16

D_MODEL = 1024
N_META = 16
CHUNK = 64
N_PAD = CHUNK - N_META
DN_HEADS = 4
DN_DK = 128
DN_DV = 128
CONV_K = 4
GLA_HEADS = 4
GLA_DK = 64
GLA_DV = 128
GLA_RANK = 16
GLA_NORMALIZER = 16.0
D_FF = 4 * D_MODEL
EPS = 1e-6

DN_QK = DN_HEADS * DN_DK
DN_V = DN_HEADS * DN_DV
GLA_QK = GLA_HEADS * GLA_DK
GLA_V = GLA_HEADS * GLA_DV
MIX_WIDTH = DN_V + GLA_V
CONV_W = 2 * DN_QK + DN_V

OFF_QKV = 0
OFF_Z = OFF_QKV + CONV_W
OFF_GQ = OFF_Z + DN_V
OFF_GK = OFF_GQ + GLA_QK
OFF_GV = OFF_GK + GLA_QK
OFF_GR = OFF_GV + GLA_V
BIG_W = OFF_GR + GLA_V
SM_W = 128
SM_BETA = 0
SM_A = DN_HEADS
SM_GLR = 2 * DN_HEADS

ROW_TILE = 512
PROJ_PIECE = 512
TAIL = 8
MIXER_BATCH = 4
MLP_PIECE = 256
FILL_SCHEDULE = (4, 4, 8, 2, 2, 2, 2, 2, 2, 2, 2, 2)
V7X_VMEM_LIMIT = 56 * 1024 * 1024


def _mm(a, b):
    return jnp.dot(a, b, preferred_element_type=F32)


def _mm_nt(a, b):
    return lax.dot_general(a, b, (((1,), (1,)), ((), ())), preferred_element_type=F32)


def _mm_tn(a, b):
    return lax.dot_general(a, b, (((0,), (0,)), ((), ())), preferred_element_type=F32)


def _softplus(x):
    return jnp.maximum(x, 0.0) + jnp.log1p(jnp.exp(-jnp.abs(x)))


def _silu(x):
    return x * jax.nn.sigmoid(x)


def _cumsum_rows(x):
    row = lax.broadcasted_iota(jnp.int32, x.shape, 0)
    s = 1
    while s < x.shape[0]:
        x = x + jnp.where(row >= s, pltpu.roll(x, s, 0), 0.0)
        s *= 2
    return x


def _headwise(x, width, fn):
    outs = []
    for h in range(x.shape[1] // width):
        xs = x[:, h * width:(h + 1) * width]
        outs.append(xs * fn(jnp.sum(xs * xs, axis=-1, keepdims=True)))
    return jnp.concatenate(outs, axis=1)


def _bcast_heads(cols, width):
    c, n = cols[0].shape[0], len(cols)
    if width % 128 == 0:
        return jnp.concatenate([jnp.broadcast_to(col, (c, width)) for col in cols], axis=1)
    lane = lax.broadcasted_iota(jnp.int32, (c, n * width), 1)
    out = jnp.broadcast_to(cols[-1], (c, n * width))
    for h in reversed(range(n - 1)):
        out = jnp.where(lane < (h + 1) * width, jnp.broadcast_to(cols[h], (c, n * width)), out)
    return out


def _block_diag(x16, mask_ref):
    return jnp.concatenate([x16] * 4, axis=0) * mask_ref[...]


def _in_proj_kernel(n_pad, tm, tiles_per_seq, x_ref, g_ref, wbig_ref, wsm_ref, cw_ref, tail0_ref,
                    big_ref, sm_ref, tail_ref, carry_ref):
    t = pl.program_id(0)
    slot = t % 2

    @pl.when(t % tiles_per_seq == 0)
    def _():
        carry_ref[slot] = tail0_ref[...]

    x = x_ref[...]
    h = x * lax.rsqrt(jnp.mean(x * x, axis=-1, keepdims=True) + EPS) * g_ref[...]
    if n_pad:
        row = t * tm + lax.broadcasted_iota(jnp.int32, h.shape, 0)
        h = jnp.where(row >= n_pad, h, 0.0)
    h16 = h.astype(BF16)
    nb = PROJ_PIECE
    l2 = lambda s: lax.rsqrt(s + EPS)
    n_pieces = BIG_W // nb
    n_conv = OFF_Z // nb
    order = [j for pair in zip(range(n_conv), range(n_conv, 2 * n_conv)) for j in pair]
    order += list(range(2 * n_conv, n_pieces))
    proj = lambda j: _mm(h16, wbig_ref[:, j * nb:(j + 1) * nb])
    ahead = proj(order[0])
    for pos, j in enumerate(order):
        cols = slice(j * nb, (j + 1) * nb)
        pr = ahead
        if pos + 1 < n_pieces:
            ahead = proj(order[pos + 1])
        if cols.stop <= OFF_Z:
            win = jnp.concatenate([carry_ref[slot, :, cols], pr], axis=0)
            last = pr[tm - TAIL:tm]
            carry_ref[1 - slot, :, cols] = last
            tail_ref[:, cols] = last
            y = pr * cw_ref[CONV_K - 1:CONV_K, cols]
            for d in range(1, CONV_K):
                y = y + pltpu.roll(win, d, 0)[TAIL:] * cw_ref[CONV_K - 1 - d:CONV_K - d, cols]
            pr = _silu(y)
            if cols.stop <= DN_QK:
                pr = _headwise(pr, DN_DK, l2) * (DN_DK ** -0.5)
            elif cols.stop <= 2 * DN_QK:
                pr = _headwise(pr, DN_DK, l2)
        elif cols.stop <= OFF_GQ or cols.start >= OFF_GR:
            pr = _silu(pr)
        big_ref[:, cols] = pr.astype(BF16)
    sm_ref[...] = _mm(h16, wsm_ref[...])


def _in_proj(x2d, norm_g, w_big, w_sm, conv_w, tail0, seq_len, n_pad):
    rows = x2d.shape[0]
    tm = min(ROW_TILE, rows)
    assert rows % tm == 0 and seq_len % tm == 0 and DN_QK % PROJ_PIECE == 0 and OFF_Z % PROJ_PIECE == 0
    const = lambda a: pl.BlockSpec(a.shape, lambda i: (0,) * a.ndim)
    return pl.pallas_call(
        functools.partial(_in_proj_kernel, n_pad, tm, seq_len // tm),
        grid=(rows // tm,),
        in_specs=[
            pl.BlockSpec((tm, D_MODEL), lambda i: (i, 0)),
            const(norm_g), const(w_big), const(w_sm), const(conv_w), const(tail0),
        ],
        out_specs=[
            pl.BlockSpec((tm, BIG_W), lambda i: (i, 0)),
            pl.BlockSpec((tm, SM_W), lambda i: (i, 0)),
            pl.BlockSpec((TAIL, CONV_W), lambda i: (0, 0)),
        ],
        out_shape=[
            jax.ShapeDtypeStruct((rows, BIG_W), BF16),
            jax.ShapeDtypeStruct((rows, SM_W), F32),
            jax.ShapeDtypeStruct((TAIL, CONV_W), F32),
        ],
        scratch_shapes=[pltpu.VMEM((2, TAIL, CONV_W), F32)],
        compiler_params=pltpu.CompilerParams(
            dimension_semantics=("arbitrary",), vmem_limit_bytes=V7X_VMEM_LIMIT),
        name="in_proj",
    )(x2d, norm_g, w_big, w_sm, conv_w, tail0)


def _stage_gates(st, n_pad, row_base, sm_ref, alog_ref, dtb_ref, w2_ref, gb_ref, lowm_ref):
    C = CHUNK
    rb, rows = st["rb"], st["rows"]
    tril, strict = lowm_ref[0], lowm_ref[1]
    sm = sm_ref[rb, rows, :]
    beta_f = jax.nn.sigmoid(sm)
    g_f = -jnp.exp(alog_ref[...]) * _softplus(sm + dtb_ref[...])
    valid = lambda width: (row_base + st["r0"] + lax.broadcasted_iota(jnp.int32, (C, width), 0)) >= n_pad
    if n_pad:
        g_f = jnp.where(valid(SM_W), g_f, 0.0)
    beta_cols = [beta_f[:, SM_BETA + h:SM_BETA + h + 1] for h in range(DN_HEADS)]
    g_cols = [g_f[:, SM_A + h:SM_A + h + 1] for h in range(DN_HEADS)]
    st["beta64"] = _bcast_heads(beta_cols, C)
    st["beta128"] = _bcast_heads(beta_cols, DN_DK)
    dlog = _cumsum_rows(_bcast_heads(g_cols, C) * strict)
    st["decay"] = jnp.exp(dlog) * tril
    gc_cols = [dlog[:, C * h:C * h + 1] + g_cols[h][0:1, :] for h in range(DN_HEADS)]
    gc128 = _bcast_heads(gc_cols, DN_DK)
    gcl = gc128[C - 1:C, :]
    st["egc"] = jnp.exp(gc128)
    st["ekd"] = jnp.exp(gcl - gc128)
    st["elast"] = jnp.exp(gcl)
    zg = _mm(sm.astype(BF16), w2_ref[...]) + gb_ref[...]
    gl = -_softplus(-zg) * (1.0 / GLA_NORMALIZER)
    if n_pad:
        gl = jnp.where(valid(GLA_QK), gl, 0.0)
    st["b"] = _cumsum_rows(gl)


def _stage_qk(st, p_ref, m512_ref, m256_ref):
    C = CHUNK
    rb, rows = st["rb"], st["rows"]
    q = p_ref[rb, rows, OFF_QKV:OFF_QKV + DN_QK].astype(F32)
    k = p_ref[rb, rows, OFF_QKV + DN_QK:OFF_QKV + 2 * DN_QK].astype(F32)
    v = p_ref[rb, rows, OFF_QKV + 2 * DN_QK:OFF_QKV + CONV_W].astype(F32)
    kbd = _block_diag(k.astype(BF16), m512_ref)
    st["kkqk"] = _mm_nt(jnp.concatenate([k, q], axis=0).astype(BF16), kbd)
    beta128, egc = st.pop("beta128"), st.pop("egc")
    st["vb16"] = (v * beta128).astype(BF16)
    st["kbg16"] = (k * beta128 * egc).astype(BF16)
    st["qg16"] = (q * egc).astype(BF16)
    st["kd16"] = (k * st.pop("ekd")).astype(BF16)

    b = st.pop("b")
    gq = p_ref[rb, rows, OFF_GQ:OFF_GQ + GLA_QK].astype(F32) * (GLA_DK ** -0.5)
    gk = p_ref[rb, rows, OFF_GK:OFF_GK + GLA_QK].astype(F32)
    bref = b[C // 2:C // 2 + 1, :]
    blast = b[C - 1:C, :]
    qi = gq * jnp.exp(b - bref)
    ki = gk * jnp.exp(bref - b)
    st["a_gla"] = _mm_nt(qi.astype(BF16), _block_diag(ki.astype(BF16), m256_ref))
    st["qg_g16"] = (gq * jnp.exp(b)).astype(BF16)
    st["kd_g16"] = (gk * jnp.exp(blast - b)).astype(BF16)
    st["glast"] = jnp.exp(blast)


def _stage_intra(st, p_ref, m512_ref, lowm_ref):
    C = CHUNK
    tril, strict, eye = lowm_ref[0], lowm_ref[1], lowm_ref[2]
    kkqk, decay = st.pop("kkqk"), st.pop("decay")
    lmat = kkqk[:C] * st.pop("beta64") * decay * strict
    st["attn16"] = (kkqk[C:] * decay).astype(BF16)
    st["pj"] = -lmat
    st["t"] = eye - lmat
    gv16 = p_ref[st["rb"], st["rows"], OFF_GV:OFF_GV + GLA_V]
    st["o_gla_intra"] = _mm((st.pop("a_gla") * tril).astype(BF16), _block_diag(gv16, m512_ref))


def _inverse_step(j, n_sq, st, m256_ref):
    C = CHUNK
    pbd = _block_diag(st["pj"].astype(BF16), m256_ref)
    if j == 0:
        st["pj"] = _mm(st["pj"].astype(BF16), pbd)
    elif j < n_sq:
        both = _mm(jnp.concatenate([st["pj"], st["t"]], axis=0).astype(BF16), pbd)
        st["pj"] = both[:C]
        st["t"] = st["t"] + both[C:]
    else:
        st["t"] = st["t"] + _mm(st["t"].astype(BF16), pbd)


def _stage_uw(st, m512_ref):
    t16 = st.pop("t").astype(BF16)
    st["u"] = _mm(t16, _block_diag(st.pop("vb16"), m512_ref))
    st["w"] = _mm(t16, _block_diag(st.pop("kbg16"), m512_ref))


def _stage_state_read(st, p_ref, mgla_ref, sdn_ref, sgl_ref):
    C = CHUNK
    rb = st["rb"]
    wq = jnp.concatenate([st.pop("w").astype(BF16), st.pop("qg16")], axis=0)
    ws, qs = [], []
    for h in range(DN_HEADS):
        res = _mm(wq[:, DN_DK * h:DN_DK * (h + 1)], sdn_ref[rb, h].astype(BF16))
        ws.append(res[:C])
        qs.append(res[C:])
    st["ws"], st["qs"] = ws, qs
    gv16 = p_ref[rb, st["rows"], OFF_GV:OFF_GV + GLA_V]
    st["o_gla"] = st.pop("o_gla_intra") + _mm_nt(st.pop("qg_g16"), sgl_ref[rb].astype(BF16))
    sgl_ref[rb] = sgl_ref[rb] * st.pop("glast") + _mm_tn(gv16, st.pop("kd_g16")) * mgla_ref[...]


def _stage_state_write(st, m512_ref, sdn_ref):
    rb = st["rb"]
    v_new = st.pop("u") - jnp.concatenate(st.pop("ws"), axis=1)
    vn16 = v_new.astype(BF16)
    st["o_dn"] = jnp.concatenate(st.pop("qs"), axis=1) + _mm(st.pop("attn16"), _block_diag(vn16, m512_ref))
    kd16, elast = st.pop("kd16"), st.pop("elast")
    for h in range(DN_HEADS):
        hs = slice(DN_DV * h, DN_DV * (h + 1))
        sdn_ref[rb, h] = sdn_ref[rb, h] * elast[:, hs] + _mm_tn(kd16[:, hs], vn16[:, hs])


def _stage_out(st, p_ref, dng_ref, glg_ref, mix_ref):
    rb, rows = st["rb"], st["rows"]
    rms = lambda s: lax.rsqrt(s * (1.0 / DN_DV) + EPS)
    z_act = p_ref[rb, rows, OFF_Z:OFF_Z + DN_V].astype(F32)
    mix_ref[rb, rows, 0:DN_V] = (_headwise(st.pop("o_dn"), DN_DV, rms) * dng_ref[...] * z_act).astype(BF16)
    r_act = p_ref[rb, rows, OFF_GR:OFF_GR + GLA_V].astype(F32)
    rms_g = lambda s: lax.rsqrt(s * (1.0 / GLA_DV) + EPS)
    mix_ref[rb, rows, DN_V:MIX_WIDTH] = (
        _headwise(st.pop("o_gla"), GLA_DV, rms_g) * glg_ref[...] * r_act).astype(BF16)


def _mixer_step(i, n_pad, NB, p_ref, sm_ref, consts, sdn0_ref, sgl0_ref, mix_ref, sdn_ref, sgl_ref, fillers=()):
    alog_ref, dtb_ref, dng_ref, glg_ref, w2_ref, gb_ref, m512_ref, m256_ref, mgla_ref, lowm_ref = consts
    fill = iter(fillers)
    counts = iter(FILL_SCHEDULE)

    def gap():
        for _ in range(next(counts)):
            piece = next(fill, None)
            if piece is not None:
                piece()

    @pl.when(i == 0)
    def _():
        for rb in range(NB):
            sdn_ref[rb] = sdn0_ref[0]
            sgl_ref[rb] = sgl0_ref[0]

    sts = [dict(rb=rb, r0=0, rows=slice(0, CHUNK)) for rb in range(NB)]
    gap()
    for st in sts:
        _stage_gates(st, n_pad, i * CHUNK, sm_ref, alog_ref, dtb_ref, w2_ref, gb_ref, lowm_ref)
    gap()
    for st in sts:
        _stage_qk(st, p_ref, m512_ref, m256_ref)
    gap()
    for st in sts:
        _stage_intra(st, p_ref, m512_ref, lowm_ref)
    n_sq = CHUNK.bit_length() - 2
    for j in range(n_sq + 1):
        gap()
        for st in sts:
            _inverse_step(j, n_sq, st, m256_ref)
    gap()
    for st in sts:
        _stage_uw(st, m512_ref)
    gap()
    for st in sts:
        _stage_state_read(st, p_ref, mgla_ref, sdn_ref, sgl_ref)
    gap()
    for st in sts:
        _stage_state_write(st, m512_ref, sdn_ref)
    for piece in fill:
        piece()
    for st in sts:
        _stage_out(st, p_ref, dng_ref, glg_ref, mix_ref)


N_MIXER_CONSTS = 10


def _mixer_kernel(n_pad, NB, *refs):
    p_ref, sm_ref = refs[:2]
    consts = refs[2:2 + N_MIXER_CONSTS]
    sdn0_ref, sgl0_ref, mix_ref, sdn_ref, sgl_ref = refs[2 + N_MIXER_CONSTS:]
    _mixer_step(pl.program_id(1), n_pad, NB, p_ref, sm_ref, consts, sdn0_ref, sgl0_ref, mix_ref, sdn_ref, sgl_ref)


def _mixer_masks():
    r = jnp.arange(4 * CHUNK)[:, None]
    m512 = (r // CHUNK == jnp.arange(DN_V)[None, :] // DN_DV).astype(BF16)
    m256 = (r // CHUNK == jnp.arange(4 * CHUNK)[None, :] // CHUNK).astype(BF16)
    mgla = (jnp.arange(GLA_V)[:, None] // GLA_DV == jnp.arange(GLA_QK)[None, :] // GLA_DK).astype(F32)
    c = jnp.arange(CHUNK)[:, None]
    s = jnp.arange(4 * CHUNK)[None, :] % CHUNK
    lowm = jnp.stack([c >= s, c > s, c == s]).astype(F32)
    return m512, m256, mgla, lowm


def _mixer(big, sm, consts, sdn0, sgl0, n_pad):
    B, T, _ = big.shape
    NB = min(MIXER_BATCH, B)
    R = CHUNK
    assert B % NB == 0 and T % R == 0 and R % CHUNK == 0
    masks = _mixer_masks()
    const_arrays = list(consts) + list(masks)
    full = lambda a: pl.BlockSpec(a.shape, lambda b, i: (0,) * a.ndim)
    return pl.pallas_call(
        functools.partial(_mixer_kernel, n_pad, NB),
        grid=(B // NB, T // R),
        in_specs=[
            pl.BlockSpec((NB, R, BIG_W), lambda b, i: (b, i, 0)),
            pl.BlockSpec((NB, R, SM_W), lambda b, i: (b, i, 0)),
        ] + [full(a) for a in const_arrays] + [full(sdn0), full(sgl0)],
        out_specs=[
            pl.BlockSpec((NB, R, MIX_WIDTH), lambda b, i: (b, i, 0)),
            pl.BlockSpec((NB, DN_HEADS, DN_DK, DN_DV), lambda b, i: (b, 0, 0, 0)),
            pl.BlockSpec((NB, GLA_V, GLA_QK), lambda b, i: (b, 0, 0)),
        ],
        out_shape=[
            jax.ShapeDtypeStruct((B, T, MIX_WIDTH), BF16),
            jax.ShapeDtypeStruct((B, DN_HEADS, DN_DK, DN_DV), F32),
            jax.ShapeDtypeStruct((B, GLA_V, GLA_QK), F32),
        ],
        compiler_params=pltpu.CompilerParams(
            dimension_semantics=("arbitrary", "arbitrary"), vmem_limit_bytes=V7X_VMEM_LIMIT),
        name="mixer",
    )(big, sm, *const_arrays, sdn0, sgl0)


def _mlp_pieces(NB, x_ref, mixp_ref, wout_ref, g2_ref, wup_ref, wdown_ref, gf_ref, o_ref):
    env = {}
    kb = 1024
    nq = D_MODEL // MLP_PIECE

    def cols(n):
        return slice(n * MLP_PIECE, (n + 1) * MLP_PIECE)

    def p_out(n):
        if n == 0:
            env["mix"] = jnp.concatenate([mixp_ref[rb] for rb in range(NB)], axis=0)
            env["x1"] = []
        x = jnp.concatenate([x_ref[rb, :, cols(n)] for rb in range(NB)], axis=0)
        env["x1"].append(x + _mm(env["mix"], wout_ref[:, cols(n)]))
        if n == nq - 1:
            x1 = jnp.concatenate(env.pop("x1"), axis=1)
            env["h2"] = (x1 * lax.rsqrt(jnp.mean(x1 * x1, axis=-1, keepdims=True) + EPS) * g2_ref[...]).astype(BF16)
            env["acc"] = [x1[:, cols(m)] for m in range(nq)]
            env.pop("mix")

    def p_up(j, n):
        if n == 0:
            env["a"] = []
        a = jnp.maximum(_mm(env["h2"], wup_ref[:, j * kb + n * MLP_PIECE:j * kb + (n + 1) * MLP_PIECE]), 0.0)
        env["a"].append((a * a).astype(BF16))
        if n == nq - 1:
            env["a"] = jnp.concatenate(env["a"], axis=1)

    def p_down(j, n):
        env["acc"][n] = env["acc"][n] + _mm(env["a"], wdown_ref[j * kb:(j + 1) * kb, cols(n)])

    def p_final():
        acc = jnp.concatenate(env.pop("acc"), axis=1)
        out = acc * lax.rsqrt(jnp.mean(acc * acc, axis=-1, keepdims=True) + EPS) * gf_ref[...]
        for rb in range(NB):
            o_ref[rb] = out[rb * CHUNK:(rb + 1) * CHUNK]

    pieces = [functools.partial(p_out, n) for n in range(nq)]
    for j in range(D_FF // kb):
        pieces += [functools.partial(p_up, j, n) for n in range(nq)]
        pieces += [functools.partial(p_down, j, n) for n in range(nq)]
    return pieces + [p_final]


def _mixer_mlp_kernel(NB, NT, *refs):
    p_ref, sm_ref, x_ref = refs[:3]
    consts = refs[3:3 + N_MIXER_CONSTS]
    (sdn0_ref, sgl0_ref, wout_ref, g2_ref, wup_ref, wdown_ref, gf_ref,
     o_ref, mix_ref, sdn_ref, sgl_ref) = refs[3 + N_MIXER_CONSTS:]
    s = pl.program_id(0)
    i = s % NT
    half = i % 2

    @pl.when(s == 0)
    def _():
        mix_ref[1] = jnp.zeros(mix_ref.shape[1:], mix_ref.dtype)

    pieces = _mlp_pieces(NB, x_ref, mix_ref.at[1 - half], wout_ref, g2_ref, wup_ref, wdown_ref, gf_ref, o_ref)
    _mixer_step(i, 0, NB, p_ref, sm_ref, consts, sdn0_ref, sgl0_ref, mix_ref.at[half], sdn_ref, sgl_ref,
                fillers=pieces)


def _mixer_mlp(big, sm, x, consts, sdn0, sgl0, w_out, norm2_g, w_up, w_down, final_g):
    B, T, _ = big.shape
    NB = min(MIXER_BATCH, B)
    NT = T // CHUNK
    assert B % NB == 0 and T % CHUNK == 0 and NT % 2 == 0
    last = (B // NB) * NT - 1
    const_arrays = list(consts) + list(_mixer_masks())
    assert len(const_arrays) == N_MIXER_CONSTS
    once = lambda a: pl.BlockSpec(a.shape, lambda s: (0,) * a.ndim, pipeline_mode=pl.Buffered(1))
    cur = lambda s: (jnp.minimum(s, last) // NT, jnp.minimum(s, last) % NT, 0)
    prev = lambda s: (jnp.maximum(s - 1, 0) // NT, jnp.maximum(s - 1, 0) % NT, 0)
    weights = [w_out, norm2_g, w_up, w_down, final_g]
    return pl.pallas_call(
        functools.partial(_mixer_mlp_kernel, NB, NT),
        grid=(last + 2,),
        in_specs=[
            pl.BlockSpec((NB, CHUNK, BIG_W), cur),
            pl.BlockSpec((NB, CHUNK, SM_W), cur),
            pl.BlockSpec((NB, CHUNK, D_MODEL), prev),
        ] + [once(a) for a in const_arrays + [sdn0, sgl0] + weights],
        out_specs=pl.BlockSpec((NB, CHUNK, D_MODEL), prev),
        out_shape=jax.ShapeDtypeStruct((B, T, D_MODEL), F32),
        scratch_shapes=[
            pltpu.VMEM((2, NB, CHUNK, MIX_WIDTH), BF16),
            pltpu.VMEM((NB, DN_HEADS, DN_DK, DN_DV), F32),
            pltpu.VMEM((NB, GLA_V, GLA_QK), F32),
        ],
        compiler_params=pltpu.CompilerParams(
            dimension_semantics=("arbitrary",), vmem_limit_bytes=V7X_VMEM_LIMIT),
        name="mixer_mlp",
    )(big, sm, x, *const_arrays, sdn0, sgl0, *weights)


def _pad_lanes(row, offset, width):
    return jnp.zeros((1, width), F32).at[0, offset:offset + row.shape[0]].set(row.astype(F32))


def kernel(x, meta_tokens, norm1_g, w_in, conv_w, a_log, dt_bias, dn_norm_g, gla_w2, gla_b, gla_norm_g,
           w_out, norm2_g, w_up, w_down, final_norm_g):
    B, S, D = x.shape
    assert norm1_g.shape[0] == 1 and D == D_MODEL and S % CHUNK == 0

    offs = [0]
    for width in (DN_QK, DN_QK, DN_V, DN_V, DN_HEADS, DN_HEADS, GLA_QK, GLA_QK, GLA_V, GLA_V, GLA_RANK):
        offs.append(offs[-1] + width)
    w = w_in[0]
    col = lambda i: w[:, offs[i]:offs[i + 1]]
    w_big = jnp.concatenate([col(0), col(1), col(2), col(3), col(6), col(7), col(8), col(9)], axis=1).astype(BF16)
    w_sm = jnp.concatenate(
        [col(4), col(5), col(10), jnp.zeros((D, SM_W - 2 * DN_HEADS - GLA_RANK), w.dtype)], axis=1).astype(BF16)

    w2_pad = jnp.zeros((SM_W, GLA_QK), F32).at[SM_GLR:SM_GLR + GLA_RANK].set(gla_w2[0]).astype(BF16)
    consts = (
        _pad_lanes(a_log[0], SM_A, SM_W),
        _pad_lanes(dt_bias[0], SM_A, SM_W),
        jnp.tile(dn_norm_g[0].astype(F32), DN_HEADS)[None, :],
        jnp.tile(gla_norm_g[0].astype(F32), GLA_HEADS)[None, :],
        w2_pad,
        gla_b[0].astype(F32)[None, :],
    )
    g1 = norm1_g[0].astype(F32)[None, :]

    prefix = jnp.concatenate([jnp.zeros((N_PAD, D), x.dtype), meta_tokens.astype(x.dtype)], axis=0)
    zeros = lambda *shape: jnp.zeros(shape, F32)
    cw = conv_w[0].astype(F32)
    big0, sm0, tail0 = _in_proj(prefix, g1, w_big, w_sm, cw, zeros(TAIL, CONV_W), CHUNK, N_PAD)
    _, sdn0, sgl0 = _mixer(big0[None], sm0[None], consts,
                           zeros(1, DN_HEADS, DN_DK, DN_DV), zeros(1, GLA_V, GLA_QK), N_PAD)

    big, sm, _ = _in_proj(x.reshape(B * S, D), g1, w_big, w_sm, cw, tail0, S, 0)
    return _mixer_mlp(big.reshape(B, S, BIG_W), sm.reshape(B, S, SM_W), x, consts, sdn0, sgl0,
                      w_out[0].astype(BF16), norm2_g[0].astype(F32)[None, :], w_up[0].astype(BF16),
                      w_down[0].astype(BF16), final_norm_g.astype(F32)[None, :])
```
